```python
import jax, jax.numpy as jnp
from jax import lax
import numpy as np

D_MODEL = 1024
BATCH = 4
SEQ = 4096
DEPTH = 1

D_FF = 2816
A_HEADS = 8
A_HEAD_DIM = 64
A_WIDTH = A_HEADS * A_HEAD_DIM
MOBA_BLOCK = 256
MOBA_TOPK = 3
Q_CHUNK = 64
G_GROUPS = 8
G_CHUNK = 128
G_WIDTH = 512
G_GROUP_DIM = G_WIDTH // G_GROUPS
IN_WIDTH = 3 * A_WIDTH + 2 * G_WIDTH
N_BRANCHES = 2
EPS = 1e-6

kernel_name = "hybrid_moba_gmlp_macaron_block"


def rmsnorm(x, g):
    xf = x.astype(jnp.float32)
    y = xf * lax.rsqrt(jnp.mean(xf * xf, axis=-1, keepdims=True) + EPS)
    return (y * g.astype(jnp.float32)).astype(x.dtype)


def layernorm(x, g, b):
    xf = x.astype(jnp.float32)
    mu = jnp.mean(xf, axis=-1, keepdims=True)
    var = jnp.mean(jnp.square(xf - mu), axis=-1, keepdims=True)
    y = (xf - mu) * lax.rsqrt(var + EPS)
    return (y * g.astype(jnp.float32) + b.astype(jnp.float32)).astype(x.dtype)


def swiglu(x, w_gate, w_up, w_down):
    return (jax.nn.silu(x @ w_gate) * (x @ w_up)) @ w_down


def moba_attention(q, k, v):
    B, H, S, Dh = q.shape
    n_blocks = -(-S // MOBA_BLOCK)
    s_pad = n_blocks * MOBA_BLOCK
    k_sel = min(MOBA_TOPK, n_blocks)
    pad = ((0, 0), (0, 0), (0, s_pad - S), (0, 0))
    kp = jnp.pad(k, pad)
    vp = jnp.pad(v, pad)
    kb = kp.reshape(B, H, n_blocks, MOBA_BLOCK, Dh)
    vb = vp.reshape(B, H, n_blocks, MOBA_BLOCK, Dh)
    k_mean = jnp.mean(kb.astype(jnp.float32), axis=3)
    scale = Dh ** -0.5
    b_idx = jnp.arange(B)[:, None, None, None]
    h_idx = jnp.arange(H)[None, :, None, None]
    block_ids = jnp.arange(n_blocks)

    def one_chunk(c):
        start = c * Q_CHUNK
        qc = lax.dynamic_slice_in_dim(q, start, Q_CHUNK, axis=2) * scale
        own = start // MOBA_BLOCK
        gate = jnp.einsum('bhqd,bhnd->bhqn', qc.astype(jnp.float32), k_mean)
        gate = jnp.where(block_ids < own, gate, -jnp.inf)
        top_s, top_i = lax.top_k(gate, k_sel)
        valid = jnp.isfinite(top_s)
        kg = kb[b_idx, h_idx, top_i]
        vg = vb[b_idx, h_idx, top_i]
        s_sel = jnp.einsum('bhqd,bhqkpd->bhqkp', qc, kg).astype(jnp.float32)
        s_sel = jnp.where(valid[..., None], s_sel, -jnp.inf)
        k_own = lax.dynamic_slice_in_dim(kp, own * MOBA_BLOCK, MOBA_BLOCK, axis=2)
        v_own = lax.dynamic_slice_in_dim(vp, own * MOBA_BLOCK, MOBA_BLOCK, axis=2)
        s_own = jnp.einsum('bhqd,bhpd->bhqp', qc, k_own).astype(jnp.float32)
        q_pos = start + jnp.arange(Q_CHUNK)
        k_pos = own * MOBA_BLOCK + jnp.arange(MOBA_BLOCK)
        s_own = jnp.where(k_pos[None, :] <= q_pos[:, None], s_own, -jnp.inf)
        logits = jnp.concatenate([s_sel.reshape(B, H, Q_CHUNK, k_sel * MOBA_BLOCK), s_own], axis=-1)
        p = jax.nn.softmax(logits, axis=-1)
        p_sel = p[..., :k_sel * MOBA_BLOCK].reshape(B, H, Q_CHUNK, k_sel, MOBA_BLOCK).astype(v.dtype)
        p_own = p[..., k_sel * MOBA_BLOCK:].astype(v.dtype)
        return (jnp.einsum('bhqkp,bhqkpd->bhqd', p_sel, vg)
                + jnp.einsum('bhqp,bhpd->bhqd', p_own, v_own))

    outs = lax.map(one_chunk, jnp.arange(S // Q_CHUNK))
    return outs.transpose(1, 2, 0, 3, 4).reshape(B, H, S, Dh)


def chunked_spatial_gating(u, v, ln_g, ln_b, w_s, b_s):
    B, S, _ = v.shape
    v = layernorm(v, ln_g, ln_b)
    n_chunks = S // G_CHUNK
    vc = v.reshape(B, n_chunks, G_CHUNK, G_GROUPS, G_GROUP_DIM)
    causal = jnp.tril(jnp.ones((G_CHUNK, G_CHUNK), dtype=bool))
    w = jnp.where(causal[None], w_s, jnp.zeros((), w_s.dtype))
    mixed = jnp.einsum('gij,bcjgd->bcigd', w, vc) + b_s.T[None, None, :, :, None]
    return u * mixed.reshape(B, S, G_WIDTH)


def setup_inputs(seed: int = 0) -> dict:
    key = jax.random.key(seed)
    ks = jax.random.split(key, 24)
    L, D = DEPTH, D_MODEL

    def w(k, shape, fan_in, mult=1.0):
        return jax.random.normal(k, shape, jnp.float32) * (mult * fan_in ** -0.5)

    def gain(k, shape):
        return 1.0 + 0.05 * jax.random.normal(k, shape, jnp.float32)

    def bias(k, shape, s=0.02):
        return s * jax.random.normal(k, shape, jnp.float32)

    return {
        "x": jax.random.normal(ks[0], (BATCH, SEQ, D), jnp.float32),
        "ffn1_norm": gain(ks[1], (L, D)),
        "ffn1_w_gate": w(ks[2], (L, D, D_FF), D),
        "ffn1_w_up": w(ks[3], (L, D, D_FF), D),
        "ffn1_w_down": w(ks[4], (L, D_FF, D), D_FF),
        "mix_norm": gain(ks[5], (L, D)),
        "w_in": w(ks[6], (L, D, IN_WIDTH), D),
        "gmlp_ln_g": gain(ks[7], (L, G_WIDTH)),
        "gmlp_ln_b": bias(ks[8], (L, G_WIDTH)),
        "gmlp_w_s": w(ks[9], (L, G_GROUPS, G_CHUNK, G_CHUNK), G_CHUNK, 0.5),
        "gmlp_b_s": 1.0 + bias(ks[10], (L, G_GROUPS, G_CHUNK), 0.05),
        "w_branch_attn": w(ks[11], (L, A_WIDTH, D), A_WIDTH),
        "w_branch_gmlp": w(ks[12], (L, G_WIDTH, D), G_WIDTH),
        "w_gate": w(ks[13], (L, D, N_BRANCHES * D), D),
        "b_gate": bias(ks[14], (L, N_BRANCHES * D)),
        "w_out": w(ks[15], (L, D, D), D),
        "ffn2_norm": gain(ks[16], (L, D)),
        "ffn2_w_gate": w(ks[17], (L, D, D_FF), D),
        "ffn2_w_up": w(ks[18], (L, D, D_FF), D),
        "ffn2_w_down": w(ks[19], (L, D_FF, D), D_FF),
        "final_norm": gain(ks[20], (D,)),
    }


def reference(x, ffn1_norm, ffn1_w_gate, ffn1_w_up, ffn1_w_down, mix_norm, w_in,
              gmlp_ln_g, gmlp_ln_b, gmlp_w_s, gmlp_b_s, w_branch_attn, w_branch_gmlp,
              w_gate, b_gate, w_out, ffn2_norm, ffn2_w_gate, ffn2_w_up, ffn2_w_down,
              final_norm):
    B, S, D = x.shape
    h = x
    for l in range(DEPTH):
        h = h + 0.5 * swiglu(rmsnorm(h, ffn1_norm[l]), ffn1_w_gate[l], ffn1_w_up[l], ffn1_w_down[l])

        n = rmsnorm(h, mix_norm[l])
        z = n @ w_in[l]
        q = z[..., :A_WIDTH]
        k = z[..., A_WIDTH:2 * A_WIDTH]
        v = z[..., 2 * A_WIDTH:3 * A_WIDTH]
        to_heads = lambda t: t.reshape(B, S, A_HEADS, A_HEAD_DIM).transpose(0, 2, 1, 3)
        attn = moba_attention(to_heads(q), to_heads(k), to_heads(v))
        attn = attn.transpose(0, 2, 1, 3).reshape(B, S, A_WIDTH)

        zg = jax.nn.gelu(z[..., 3 * A_WIDTH:])
        gm = chunked_spatial_gating(zg[..., :G_WIDTH], zg[..., G_WIDTH:],
                                    gmlp_ln_g[l], gmlp_ln_b[l], gmlp_w_s[l], gmlp_b_s[l])

        y_attn = attn @ w_branch_attn[l]
        y_gmlp = gm @ w_branch_gmlp[l]
        gates = jax.nn.sigmoid(n @ w_gate[l] + b_gate[l])
        merged = gates[..., :D] * y_attn + gates[..., D:] * y_gmlp
        h = h + merged @ w_out[l]

        h = h + 0.5 * swiglu(rmsnorm(h, ffn2_norm[l]), ffn2_w_gate[l], ffn2_w_up[l], ffn2_w_down[l])
    return rmsnorm(h, final_norm)
```

```python
import functools

import jax
import jax.numpy as jnp
from jax import lax
from jax.experimental import pallas as pl
from jax.experimental.pallas import tpu as pltpu

D_MODEL = 1024
D_FF = 2816
A_HEADS = 8
A_HEAD_DIM = 64
A_WIDTH = A_HEADS * A_HEAD_DIM
MOBA_BLOCK = 256
MOBA_TOPK = 3
G_GROUPS = 8
G_CHUNK = 128
G_WIDTH = 512
G_GROUP_DIM = G_WIDTH // G_GROUPS
EPS = 1e-6

LANES = 128
HEAD_PAIRS = A_WIDTH // LANES
GROUP_PAIRS = G_WIDTH // LANES
MXU_COLS = 256
FF_CHUNKS = (768, 768, 768, 512)
assert sum(FF_CHUNKS) == D_FF and all(c % MXU_COLS == 0 for c in FF_CHUNKS)

TOKEN_TILE = 512
VMEM_LIMIT_BYTES = 56 * 1024 * 1024
MASKED = -1e30

F32 = jnp.float32
BF16 = jnp.bfloat16


def _dot(a, b):
    return jnp.dot(a, b, preferred_element_type=F32)


def _dot_nt(a, b):
    return lax.dot_general(a, b, (((1,), (1,)), ((), ())), preferred_element_type=F32)


def _rmsnorm(x, g):
    return x * lax.rsqrt(jnp.mean(x * x, axis=-1, keepdims=True) + EPS) * g


def _swiglu_residual(x, norm_g, wg_ref, wu_ref, wd_ref):
    n = _rmsnorm(x, norm_g).astype(BF16)
    acc = jnp.zeros(x.shape, F32)
    start = 0
    for width in FF_CHUNKS:
        gate = _dot(n, wg_ref[:, start:start + width])
        up = _dot(n, wu_ref[:, start:start + width])
        act = (gate * jax.nn.sigmoid(gate) * up).astype(BF16)
        acc = acc + _dot(act, wd_ref[start:start + width, :])
        start += width
    return x + 0.5 * acc


def _ffn1_qkv_kernel(x_ref, g1_ref, wg_ref, wu_ref, wd_ref, gmix_ref, wqkv_ref,
                     h_ref, qkv_ref, kmean_ref):
    h = _swiglu_residual(x_ref[...], g1_ref[...], wg_ref, wu_ref, wd_ref)
    h_ref[...] = h
    n = _rmsnorm(h, gmix_ref[...]).astype(BF16)
    z = _dot(n, wqkv_ref[...])
    qkv_ref[...] = z.astype(BF16)
    for j in range(TOKEN_TILE // MOBA_BLOCK):
        kb = z[j * MOBA_BLOCK:(j + 1) * MOBA_BLOCK, A_WIDTH:2 * A_WIDTH]
        kmean_ref[j] = jnp.mean(kb, axis=0, keepdims=True)


def _ffn2_out_kernel(h_ref, g2_ref, wg_ref, wu_ref, wd_ref, gfin_ref, o_ref):
    h = _swiglu_residual(h_ref[...], g2_ref[...], wg_ref, wu_ref, wd_ref)
    o_ref[...] = _rmsnorm(h, gfin_ref[...])


def _moba_kernel(q_ref, k_ref, v_ref, kmean_ref, o_ref, vt_ref, bias_ref):
    qi = pl.program_id(2)
    n_blocks = k_ref.shape[1] // MOBA_BLOCK

    @pl.when(qi == 0)
    def _():
        for n in range(n_blocks):
            vb = v_ref[0, n * MOBA_BLOCK:(n + 1) * MOBA_BLOCK, :].astype(F32)
            vt_ref[:, n * MOBA_BLOCK:(n + 1) * MOBA_BLOCK] = vb.T.astype(BF16)

    q = q_ref[0]
    lane = lax.broadcasted_iota(jnp.int32, q.shape, 1)
    kmean = kmean_ref[0].astype(BF16)
    blk_row = lax.broadcasted_iota(jnp.int32, (n_blocks, MOBA_BLOCK), 0)
    key_pos = lax.broadcasted_iota(jnp.int32, (MOBA_BLOCK, MOBA_BLOCK), 0)
    qry_pos = lax.broadcasted_iota(jnp.int32, (MOBA_BLOCK, MOBA_BLOCK), 1)
    own_start = pl.multiple_of(qi * MOBA_BLOCK, MOBA_BLOCK)

    outs = []
    for half in range(2):
        in_head = (lane >= half * A_HEAD_DIM) & (lane < (half + 1) * A_HEAD_DIM)
        qh = jnp.where(in_head, q, jnp.zeros_like(q)) * (A_HEAD_DIM ** -0.5)
        rows = slice(half * A_HEAD_DIM, (half + 1) * A_HEAD_DIM)

        valid = blk_row < qi
        gate = jnp.where(valid, _dot_nt(kmean, qh), -jnp.inf)
        rank = jnp.zeros(gate.shape, jnp.int32)
        for m in range(n_blocks):
            gm = gate[m:m + 1, :]
            ahead = (gm > gate) | ((gm == gate) & (m < blk_row))
            rank = rank + jnp.where(ahead, 1, 0)
        selected = valid & (rank < MOBA_TOPK)
        bias_ref[...] = jnp.where(selected, 0.0, MASKED).astype(F32)

        s = _dot_nt(k_ref[0, pl.ds(own_start, MOBA_BLOCK), :], qh)
        s = jnp.where(key_pos <= qry_pos, s, MASKED)
        m0 = jnp.max(s, axis=0, keepdims=True)
        p = jnp.exp(s - m0)
        l0 = jnp.sum(p, axis=0, keepdims=True)
        acc0 = _dot(vt_ref[rows, pl.ds(own_start, MOBA_BLOCK)], p.astype(BF16))

        def body(n, carry, qh=qh, rows=rows):
            m_run, l_run, acc = carry
            start = pl.multiple_of(n * MOBA_BLOCK, MOBA_BLOCK)
            s = _dot_nt(k_ref[0, pl.ds(start, MOBA_BLOCK), :], qh) + bias_ref[pl.ds(n, 1), :]
            m_new = jnp.maximum(m_run, jnp.max(s, axis=0, keepdims=True))
            alpha = jnp.exp(m_run - m_new)
            p = jnp.exp(s - m_new)
            l_new = alpha * l_run + jnp.sum(p, axis=0, keepdims=True)
            pv = _dot(vt_ref[rows, pl.ds(start, MOBA_BLOCK)], p.astype(BF16))
            return m_new, l_new, alpha * acc + pv

        _, l_fin, acc = lax.fori_loop(0, qi, body, (m0, l0, acc0))
        outs.append(acc / l_fin)

    out_t = jnp.concatenate(outs, axis=0)
    o_ref[0] = out_t.T.astype(o_ref.dtype)


def _mix_kernel(h_ref, attn_ref, gmix_ref, wuv_ref, lng_ref, lnb_ref, ws_ref, bs_ref,
                wba_ref, wbg_ref, wgate_ref, bgate_ref, wout_ref, o_ref):
    h = h_ref[...]
    n = _rmsnorm(h, gmix_ref[...]).astype(BF16)
    tm = h.shape[0]
    n_chunks = tm // G_CHUNK

    zg = jax.nn.gelu(_dot(n, wuv_ref[...]), approximate=True)
    u = zg[:, :G_WIDTH]
    v = zg[:, G_WIDTH:]
    mu = jnp.mean(v, axis=-1, keepdims=True)
    var = jnp.mean(jnp.square(v - mu), axis=-1, keepdims=True)
    v = ((v - mu) * lax.rsqrt(var + EPS) * lng_ref[...] + lnb_ref[...]).astype(BF16)

    row = lax.broadcasted_iota(jnp.int32, (G_CHUNK, G_CHUNK), 0)
    col = lax.broadcasted_iota(jnp.int32, (G_CHUNK, G_CHUNK), 1)
    lane = lax.broadcasted_iota(jnp.int32, (G_CHUNK, n_chunks * LANES), 1)
    low_half = (lane % LANES) < G_GROUP_DIM
    mixed_cols = []
    for p in range(GROUP_PAIRS):
        vp = jnp.concatenate(
            [v[c * G_CHUNK:(c + 1) * G_CHUNK, p * LANES:(p + 1) * LANES] for c in range(n_chunks)], axis=1)
        zeros = jnp.zeros_like(vp)
        rhs = jnp.concatenate([jnp.where(low_half, vp, zeros), jnp.where(low_half, zeros, vp)], axis=0)
        w_lo = jnp.where(col <= row, ws_ref[2 * p], 0.0).astype(BF16)
        w_hi = jnp.where(col <= row, ws_ref[2 * p + 1], 0.0).astype(BF16)
        mp = _dot(jnp.concatenate([w_lo, w_hi], axis=1), rhs)
        mixed_cols.append(jnp.concatenate(
            [mp[:, c * LANES:(c + 1) * LANES] for c in range(n_chunks)], axis=0))
    mixed = jnp.concatenate(mixed_cols, axis=1)
    bias = jnp.concatenate([bs_ref[...]] * n_chunks, axis=0)
    gm = (u * (mixed + bias)).astype(BF16)

    y_attn = _dot(attn_ref[...], wba_ref[...])
    y_gmlp = _dot(gm, wbg_ref[...])
    gates = jax.nn.sigmoid(_dot(n, wgate_ref[...]) + bgate_ref[...])
    merged = gates[:, :D_MODEL] * y_attn + gates[:, D_MODEL:] * y_gmlp
    o_ref[...] = h + _dot(merged.astype(BF16), wout_ref[...])


def _resident(shape):
    zeros = (0,) * len(shape)
    return pl.BlockSpec(shape, lambda *_: zeros, pipeline_mode=pl.Buffered(1))


def _row_tiles(width):
    return pl.BlockSpec((TOKEN_TILE, width), lambda i: (i, 0))


def _dense_params():
    return pltpu.CompilerParams(dimension_semantics=("parallel",), vmem_limit_bytes=VMEM_LIMIT_BYTES)


def kernel(x, ffn1_norm, ffn1_w_gate, ffn1_w_up, ffn1_w_down, mix_norm, w_in, gmlp_ln_g, gmlp_ln_b, gmlp_w_s, gmlp_b_s, w_branch_attn, w_branch_gmlp, w_gate, b_gate, w_out, ffn2_norm, ffn2_w_gate, ffn2_w_up, ffn2_w_down, final_norm):
    B, S, D = x.shape
    T = B * S
    n_blocks = S // MOBA_BLOCK
    assert D == D_MODEL and S % MOBA_BLOCK == 0 and T % TOKEN_TILE == 0
    assert TOKEN_TILE % MOBA_BLOCK == 0 and TOKEN_TILE % G_CHUNK == 0
    assert ffn1_norm.shape[0] == 1, "single-layer block"
    bf = lambda w: w[0].astype(BF16)
    vec = lambda w: w.reshape(1, -1).astype(F32)
    n_tiles = T // TOKEN_TILE
    x2 = x.reshape(T, D)

    h1, qkv, kmean = pl.pallas_call(
        _ffn1_qkv_kernel,
        grid=(n_tiles,),
        in_specs=[_row_tiles(D), _resident((1, D)), _resident((D, D_FF)), _resident((D, D_FF)),
                  _resident((D_FF, D)), _resident((1, D)), _resident((D, 3 * A_WIDTH))],
        out_specs=[_row_tiles(D), _row_tiles(3 * A_WIDTH),
                   pl.BlockSpec((TOKEN_TILE // MOBA_BLOCK, 1, A_WIDTH), lambda i: (i, 0, 0))],
        out_shape=[jax.ShapeDtypeStruct((T, D), F32), jax.ShapeDtypeStruct((T, 3 * A_WIDTH), BF16),
                   jax.ShapeDtypeStruct((T // MOBA_BLOCK, 1, A_WIDTH), F32)],
        compiler_params=_dense_params(),
        name="ffn1_qkv",
    )(x2, vec(ffn1_norm), bf(ffn1_w_gate), bf(ffn1_w_up), bf(ffn1_w_down), vec(mix_norm),
      bf(w_in[:, :, :3 * A_WIDTH]))

    qkv3 = qkv.reshape(B, S, 3 * A_WIDTH)
    attn = pl.pallas_call(
        _moba_kernel,
        grid=(B, HEAD_PAIRS, n_blocks),
        in_specs=[pl.BlockSpec((1, MOBA_BLOCK, LANES), lambda b, p, i: (b, i, p)),
                  pl.BlockSpec((1, S, LANES), lambda b, p, i: (b, 0, HEAD_PAIRS + p)),
                  pl.BlockSpec((1, S, LANES), lambda b, p, i: (b, 0, 2 * HEAD_PAIRS + p)),
                  pl.BlockSpec((1, n_blocks, LANES), lambda b, p, i: (b, 0, p))],
        out_specs=pl.BlockSpec((1, MOBA_BLOCK, LANES), lambda b, p, i: (b, i, p)),
        out_shape=jax.ShapeDtypeStruct((B, S, A_WIDTH), BF16),
        scratch_shapes=[pltpu.VMEM((LANES, S), BF16), pltpu.VMEM((n_blocks, MOBA_BLOCK), F32)],
        compiler_params=pltpu.CompilerParams(dimension_semantics=("parallel", "parallel", "arbitrary")),
        name="moba_attn",
    )(qkv3, qkv3, qkv3, kmean.reshape(B, n_blocks, A_WIDTH))

    bs_lanes = jnp.repeat(gmlp_b_s[0].T, G_GROUP_DIM, axis=1).astype(F32)
    h2 = pl.pallas_call(
        _mix_kernel,
        grid=(n_tiles,),
        in_specs=[_row_tiles(D), _row_tiles(A_WIDTH), _resident((1, D)), _resident((D, 2 * G_WIDTH)),
                  _resident((1, G_WIDTH)), _resident((1, G_WIDTH)),
                  _resident((G_GROUPS, G_CHUNK, G_CHUNK)), _resident((G_CHUNK, G_WIDTH)),
                  _resident((A_WIDTH, D)), _resident((G_WIDTH, D)), _resident((D, 2 * D)),
                  _resident((1, 2 * D)), _resident((D, D))],
        out_specs=_row_tiles(D),
        out_shape=jax.ShapeDtypeStruct((T, D), F32),
        compiler_params=_dense_params(),
        name="mix",
    )(h1, attn.reshape(T, A_WIDTH), vec(mix_norm), bf(w_in[:, :, 3 * A_WIDTH:]), vec(gmlp_ln_g),
      vec(gmlp_ln_b), gmlp_w_s[0].astype(F32), bs_lanes, bf(w_branch_attn), bf(w_branch_gmlp),
      bf(w_gate), vec(b_gate), bf(w_out))

    out = pl.pallas_call(
        _ffn2_out_kernel,
        grid=(n_tiles,),
        in_specs=[_row_tiles(D), _resident((1, D)), _resident((D, D_FF)), _resident((D, D_FF)),
                  _resident((D_FF, D)), _resident((1, D))],
        out_specs=_row_tiles(D),
        out_shape=jax.ShapeDtypeStruct((T, D), F32),
        compiler_params=_dense_params(),
        name="ffn2_out",
    )(h2, vec(ffn2_norm), bf(ffn2_w_gate), bf(ffn2_w_up), bf(ffn2_w_down), vec(final_norm))
    return out.reshape(B, S, D)
```

```python
import functools

import jax
import jax.numpy as jnp
from jax import lax
from jax.experimental import pallas as pl
from jax.experimental.pallas import tpu as pltpu

D_MODEL = 1024
D_FF = 2816
A_HEADS = 8
A_HEAD_DIM = 64
A_WIDTH = A_HEADS * A_HEAD_DIM
MOBA_BLOCK = 256
MOBA_TOPK = 3
G_GROUPS = 8
G_CHUNK = 128
G_WIDTH = 512
G_GROUP_DIM = G_WIDTH // G_GROUPS
EPS = 1e-6

LANES = 128
HEAD_PAIRS = A_WIDTH // LANES
GROUP_PAIRS = G_WIDTH // LANES
MXU_COLS = 256
FF_CHUNKS = (768, 768, 768, 512)
assert sum(FF_CHUNKS) == D_FF and all(c % MXU_COLS == 0 for c in FF_CHUNKS)

TOKEN_TILE = 512
VMEM_LIMIT_BYTES = 56 * 1024 * 1024
MASKED = -1e30
F32 = jnp.float32
BF16 = jnp.bfloat16


def _dot(a, b):
    return jnp.dot(a, b, preferred_element_type=F32)


def _dot_nt(a, b):
    return lax.dot_general(a, b, (((1,), (1,)), ((), ())), preferred_element_type=F32)


def _rmsnorm(x, g):
    return x * lax.rsqrt(jnp.mean(x * x, axis=-1, keepdims=True) + EPS) * g


def _swiglu_residual(x, norm_g, wg_ref, wu_ref, wd_ref):
    n = _rmsnorm(x, norm_g).astype(BF16)
    acc = jnp.zeros(x.shape, F32)
    start = 0
    for width in FF_CHUNKS:
        gate = _dot(n, wg_ref[:, start:start + width])
        up = _dot(n, wu_ref[:, start:start + width])
        act = (gate * jax.nn.sigmoid(gate) * up).astype(BF16)
        acc = acc + _dot(act, wd_ref[start:start + width, :])
        start += width
    return x + 0.5 * acc


def _ffn1_qkv_kernel(x_ref, g1_ref, wg_ref, wu_ref, wd_ref, gmix_ref, wqkv_ref,
                     h_ref, qkv_ref, kmean_ref):
    h = _swiglu_residual(x_ref[...], g1_ref[...], wg_ref, wu_ref, wd_ref)
    h_ref[...] = h
    n = _rmsnorm(h, gmix_ref[...]).astype(BF16)
    z = _dot(n, wqkv_ref[...])
    qkv_ref[...] = z.astype(BF16)
    for j in range(TOKEN_TILE // MOBA_BLOCK):
        kb = z[j * MOBA_BLOCK:(j + 1) * MOBA_BLOCK, A_WIDTH:2 * A_WIDTH]
        kmean_ref[j] = jnp.mean(kb, axis=0, keepdims=True)


def _ffn2_out_kernel(h_ref, g2_ref, wg_ref, wu_ref, wd_ref, gfin_ref, o_ref):
    h = _swiglu_residual(h_ref[...], g2_ref[...], wg_ref, wu_ref, wd_ref)
    o_ref[...] = _rmsnorm(h, gfin_ref[...])


def _moba_kernel(q_ref, k_ref, v_ref, kmean_ref, o_ref, vt_ref, qh_ref, bias_ref, s_ref, p_ref):
    qi = pl.program_id(1)
    n_blocks = k_ref.shape[1] // MOBA_BLOCK

    @pl.when(qi == 0)
    def _():
        for n in range(n_blocks):
            vb = v_ref[0, n * MOBA_BLOCK:(n + 1) * MOBA_BLOCK, :].astype(F32)
            vt_ref[:, n * MOBA_BLOCK:(n + 1) * MOBA_BLOCK] = vb.T.astype(BF16)

    lane = lax.broadcasted_iota(jnp.int32, (MOBA_BLOCK, LANES), 1)
    blk_row = lax.broadcasted_iota(jnp.int32, (n_blocks, MOBA_BLOCK), 0)
    key_pos = lax.broadcasted_iota(jnp.int32, (MOBA_BLOCK, MOBA_BLOCK), 0)
    qry_pos = lax.broadcasted_iota(jnp.int32, (MOBA_BLOCK, MOBA_BLOCK), 1)
    own_start = pl.multiple_of(qi * MOBA_BLOCK, MOBA_BLOCK)
    valid = blk_row < qi

    def head_slices(h):
        pair, half = divmod(h, 2)
        return (slice(pair * LANES, (pair + 1) * LANES),
                slice(h * A_HEAD_DIM, (h + 1) * A_HEAD_DIM), half)

    init = []
    for h in range(A_HEADS):
        cols, rows, half = head_slices(h)
        q = q_ref[0, :, cols]
        in_head = (lane >= half * A_HEAD_DIM) & (lane < (half + 1) * A_HEAD_DIM)
        qh = jnp.where(in_head, q, jnp.zeros_like(q)) * (A_HEAD_DIM ** -0.5)
        qh_ref[h] = qh

        kmean = kmean_ref[0, :, cols].astype(BF16)
        gate = jnp.where(valid, _dot_nt(kmean, qh), -jnp.inf)
        rank = jnp.zeros(gate.shape, jnp.int32)
        for m in range(n_blocks):
            gm = gate[m:m + 1, :]
            ahead = (gm > gate) | ((gm == gate) & (m < blk_row))
            rank = rank + jnp.where(ahead, 1, 0)
        selected = valid & (rank < MOBA_TOPK)
        bias_ref[h] = jnp.where(selected, 0.0, MASKED).astype(F32)

        s = _dot_nt(k_ref[0, pl.ds(own_start, MOBA_BLOCK), cols], qh)
        s = jnp.where(key_pos <= qry_pos, s, MASKED)
        m0 = jnp.max(s, axis=0, keepdims=True)
        p = jnp.exp(s - m0)
        l0 = jnp.sum(p, axis=0, keepdims=True)
        acc0 = _dot(vt_ref[rows, pl.ds(own_start, MOBA_BLOCK)], p.astype(BF16))
        init.append((m0, l0, acc0))

    def body(n, carry):
        start = pl.multiple_of(n * MOBA_BLOCK, MOBA_BLOCK)

        for h in range(A_HEADS):
            s_ref[h] = (_dot_nt(k_ref[0, pl.ds(start, MOBA_BLOCK), head_slices(h)[0]], qh_ref[h])
                        + bias_ref[h, pl.ds(n, 1), :])
        stats = []
        for h in range(A_HEADS):
            m_run, l_run, _ = carry[h]
            s = s_ref[h]
            m_new = jnp.maximum(m_run, jnp.max(s, axis=0, keepdims=True))
            alpha = jnp.exp(m_run - m_new)
            p = jnp.exp(s - m_new)
            p_ref[h] = p.astype(BF16)
            stats.append((m_new, alpha * l_run + jnp.sum(p, axis=0, keepdims=True), alpha))
        out = []
        for h in range(A_HEADS):
            rows = head_slices(h)[1]
            m_new, l_new, alpha = stats[h]
            pv = _dot(vt_ref[rows, pl.ds(start, MOBA_BLOCK)], p_ref[h])
            out.append((m_new, l_new, alpha * carry[h][2] + pv))
        return tuple(out)

    final = lax.fori_loop(0, qi, body, tuple(init))
    for pair in range(HEAD_PAIRS):
        out_t = jnp.concatenate([final[2 * pair + half][2] / final[2 * pair + half][1]
                                 for half in range(2)], axis=0)
        o_ref[0, :, pair * LANES:(pair + 1) * LANES] = out_t.T.astype(o_ref.dtype)


def _mix_kernel(h_ref, attn_ref, gmix_ref, wuv_ref, lng_ref, lnb_ref, ws_ref, bs_ref,
                wba_ref, wbg_ref, wgate_ref, bgate_ref, wout_ref, o_ref):
    h = h_ref[...]
    n = _rmsnorm(h, gmix_ref[...]).astype(BF16)
    tm = h.shape[0]
    n_chunks = tm // G_CHUNK

    zg = jax.nn.gelu(_dot(n, wuv_ref[...]), approximate=True)
    u = zg[:, :G_WIDTH]
    v = zg[:, G_WIDTH:]
    mu = jnp.mean(v, axis=-1, keepdims=True)
    var = jnp.mean(jnp.square(v - mu), axis=-1, keepdims=True)
    v = ((v - mu) * lax.rsqrt(var + EPS) * lng_ref[...] + lnb_ref[...]).astype(BF16)

    row = lax.broadcasted_iota(jnp.int32, (G_CHUNK, G_CHUNK), 0)
    col = lax.broadcasted_iota(jnp.int32, (G_CHUNK, G_CHUNK), 1)
    lane = lax.broadcasted_iota(jnp.int32, (G_CHUNK, n_chunks * LANES), 1)
    low_half = (lane % LANES) < G_GROUP_DIM
    mixed_cols = []
    for p in range(GROUP_PAIRS):
        vp = jnp.concatenate(
            [v[c * G_CHUNK:(c + 1) * G_CHUNK, p * LANES:(p + 1) * LANES] for c in range(n_chunks)], axis=1)
        zeros = jnp.zeros_like(vp)
        rhs = jnp.concatenate([jnp.where(low_half, vp, zeros), jnp.where(low_half, zeros, vp)], axis=0)
        w_lo = jnp.where(col <= row, ws_ref[2 * p], 0.0).astype(BF16)
        w_hi = jnp.where(col <= row, ws_ref[2 * p + 1], 0.0).astype(BF16)
        mp = _dot(jnp.concatenate([w_lo, w_hi], axis=1), rhs)
        mixed_cols.append(jnp.concatenate(
            [mp[:, c * LANES:(c + 1) * LANES] for c in range(n_chunks)], axis=0))
    mixed = jnp.concatenate(mixed_cols, axis=1)
    bias = jnp.concatenate([bs_ref[...]] * n_chunks, axis=0)
    gm = (u * (mixed + bias)).astype(BF16)

    y_attn = _dot(attn_ref[...], wba_ref[...])
    y_gmlp = _dot(gm, wbg_ref[...])
    gates = jax.nn.sigmoid(_dot(n, wgate_ref[...]) + bgate_ref[...])
    merged = gates[:, :D_MODEL] * y_attn + gates[:, D_MODEL:] * y_gmlp
    o_ref[...] = h + _dot(merged.astype(BF16), wout_ref[...])


def _resident(shape):
    zeros = (0,) * len(shape)
    return pl.BlockSpec(shape, lambda *_: zeros, pipeline_mode=pl.Buffered(1))


def _row_tiles(width):
    return pl.BlockSpec((TOKEN_TILE, width), lambda i: (i, 0))


def _dense_params():
    return pltpu.CompilerParams(dimension_semantics=("parallel",), vmem_limit_bytes=VMEM_LIMIT_BYTES)


def kernel(x, ffn1_norm, ffn1_w_gate, ffn1_w_up, ffn1_w_down, mix_norm, w_in, gmlp_ln_g, gmlp_ln_b, gmlp_w_s, gmlp_b_s, w_branch_attn, w_branch_gmlp, w_gate, b_gate, w_out, ffn2_norm, ffn2_w_gate, ffn2_w_up, ffn2_w_down, final_norm):
    B, S, D = x.shape
    T = B * S
    n_blocks = S // MOBA_BLOCK
    assert D == D_MODEL and S % MOBA_BLOCK == 0 and T % TOKEN_TILE == 0
    assert TOKEN_TILE % MOBA_BLOCK == 0 and TOKEN_TILE % G_CHUNK == 0
    assert ffn1_norm.shape[0] == 1, "single-layer block"
    bf = lambda w: w[0].astype(BF16)
    vec = lambda w: w.reshape(1, -1).astype(F32)
    n_tiles = T // TOKEN_TILE
    x2 = x.reshape(T, D)

    h1, qkv, kmean = pl.pallas_call(
        _ffn1_qkv_kernel,
        grid=(n_tiles,),
        in_specs=[_row_tiles(D), _resident((1, D)), _resident((D, D_FF)), _resident((D, D_FF)),
                  _resident((D_FF, D)), _resident((1, D)), _resident((D, 3 * A_WIDTH))],
        out_specs=[_row_tiles(D), _row_tiles(3 * A_WIDTH),
                   pl.BlockSpec((TOKEN_TILE // MOBA_BLOCK, 1, A_WIDTH), lambda i: (i, 0, 0))],
        out_shape=[jax.ShapeDtypeStruct((T, D), F32), jax.ShapeDtypeStruct((T, 3 * A_WIDTH), BF16),
                   jax.ShapeDtypeStruct((T // MOBA_BLOCK, 1, A_WIDTH), F32)],
        compiler_params=_dense_params(),
        name="ffn1_qkv",
    )(x2, vec(ffn1_norm), bf(ffn1_w_gate), bf(ffn1_w_up), bf(ffn1_w_down), vec(mix_norm),
      bf(w_in[:, :, :3 * A_WIDTH]))

    qkv3 = qkv.reshape(B, S, 3 * A_WIDTH)
    attn = pl.pallas_call(
        _moba_kernel,
        grid=(B, n_blocks),
        in_specs=[pl.BlockSpec((1, MOBA_BLOCK, A_WIDTH), lambda b, i: (b, i, 0)),
                  pl.BlockSpec((1, S, A_WIDTH), lambda b, i: (b, 0, 1)),
                  pl.BlockSpec((1, S, A_WIDTH), lambda b, i: (b, 0, 2)),
                  pl.BlockSpec((1, n_blocks, A_WIDTH), lambda b, i: (b, 0, 0))],
        out_specs=pl.BlockSpec((1, MOBA_BLOCK, A_WIDTH), lambda b, i: (b, i, 0)),
        out_shape=jax.ShapeDtypeStruct((B, S, A_WIDTH), BF16),
        scratch_shapes=[pltpu.VMEM((A_WIDTH, S), BF16),
                        pltpu.VMEM((A_HEADS, MOBA_BLOCK, LANES), BF16),
                        pltpu.VMEM((A_HEADS, n_blocks, MOBA_BLOCK), F32),
                        pltpu.VMEM((A_HEADS, MOBA_BLOCK, MOBA_BLOCK), F32),
                        pltpu.VMEM((A_HEADS, MOBA_BLOCK, MOBA_BLOCK), BF16)],
        compiler_params=pltpu.CompilerParams(dimension_semantics=("parallel", "arbitrary"),
                                             vmem_limit_bytes=VMEM_LIMIT_BYTES),
        name="moba_attn",
    )(qkv3, qkv3, qkv3, kmean.reshape(B, n_blocks, A_WIDTH))

    bs_lanes = jnp.repeat(gmlp_b_s[0].T, G_GROUP_DIM, axis=1).astype(F32)
    h2 = pl.pallas_call(
        _mix_kernel,
        grid=(n_tiles,),
        in_specs=[_row_tiles(D), _row_tiles(A_WIDTH), _resident((1, D)), _resident((D, 2 * G_WIDTH)),
                  _resident((1, G_WIDTH)), _resident((1, G_WIDTH)),
                  _resident((G_GROUPS, G_CHUNK, G_CHUNK)), _resident((G_CHUNK, G_WIDTH)),
                  _resident((A_WIDTH, D)), _resident((G_WIDTH, D)), _resident((D, 2 * D)),
                  _resident((1, 2 * D)), _resident((D, D))],
        out_specs=_row_tiles(D),
        out_shape=jax.ShapeDtypeStruct((T, D), F32),
        compiler_params=_dense_params(),
        name="mix",
    )(h1, attn.reshape(T, A_WIDTH), vec(mix_norm), bf(w_in[:, :, 3 * A_WIDTH:]), vec(gmlp_ln_g),
      vec(gmlp_ln_b), gmlp_w_s[0].astype(F32), bs_lanes, bf(w_branch_attn), bf(w_branch_gmlp),
      bf(w_gate), vec(b_gate), bf(w_out))

    out = pl.pallas_call(
        _ffn2_out_kernel,
        grid=(n_tiles,),
        in_specs=[_row_tiles(D), _resident((1, D)), _resident((D, D_FF)), _resident((D, D_FF)),
                  _resident((D_FF, D)), _resident((1, D))],
        out_specs=_row_tiles(D),
        out_shape=jax.ShapeDtypeStruct((T, D), F32),
        compiler_params=_dense_params(),
        name="ffn2_out",
    )(h2, vec(ffn2_norm), bf(ffn2_w_gate), bf(ffn2_w_up), bf(ffn2_w_down), vec(final_norm))
    return out.reshape(B, S, D)
```

```python
import jax
import jax.numpy as jnp
from jax import lax
from jax.experimental import pallas as pl
from jax.experimental.pallas import tpu as pltpu

D_MODEL = 1024
D_FF = 2816
A_HEADS = 8
A_HEAD_DIM = 64
A_WIDTH = A_HEADS * A_HEAD_DIM
MOBA_BLOCK = 256
MOBA_TOPK = 3
G_GROUPS = 8
G_CHUNK = 128
G_WIDTH = 512
G_GROUP_DIM = G_WIDTH // G_GROUPS
EPS = 1e-6

LANES = 128
BF16_ROWS = 16
V_ROWS = A_HEAD_DIM + BF16_ROWS
HEAD_PAIRS = A_WIDTH // LANES
GROUP_PAIRS = G_WIDTH // LANES
MXU_COLS = 256
FF_CHUNKS = (768, 768, 768, 512)
assert sum(FF_CHUNKS) == D_FF and all(c % MXU_COLS == 0 for c in FF_CHUNKS)

TOKEN_TILE = 512
VMEM_LIMIT_BYTES = 56 * 1024 * 1024
MASKED = -1e30
F32 = jnp.float32
BF16 = jnp.bfloat16


def _dot(a, b):
    return jnp.dot(a, b, preferred_element_type=F32)


def _rmsnorm(x, g):
    return x * lax.rsqrt(jnp.mean(x * x, axis=-1, keepdims=True) + EPS) * g


def _swiglu_residual(x, norm_g, wg_ref, wu_ref, wd_ref):
    n = _rmsnorm(x, norm_g).astype(BF16)
    acc = jnp.zeros(x.shape, F32)
    start = 0
    for width in FF_CHUNKS:
        gate = _dot(n, wg_ref[:, start:start + width])
        up = _dot(n, wu_ref[:, start:start + width])
        act = (gate * jax.nn.sigmoid(gate) * up).astype(BF16)
        acc = acc + _dot(act, wd_ref[start:start + width, :])
        start += width
    return x + 0.5 * acc


def _ffn1_qkv_kernel(x_ref, g1_ref, wg_ref, wu_ref, wd_ref, gmix_ref, wqkv_ref,
                     h_ref, qkv_ref, kmean_ref):
    h = _swiglu_residual(x_ref[...], g1_ref[...], wg_ref, wu_ref, wd_ref)
    h_ref[...] = h
    n = _rmsnorm(h, gmix_ref[...]).astype(BF16)
    z = _dot(n, wqkv_ref[...])
    qkv_ref[...] = z.astype(BF16)
    for j in range(TOKEN_TILE // MOBA_BLOCK):
        kb = z[j * MOBA_BLOCK:(j + 1) * MOBA_BLOCK, A_WIDTH:2 * A_WIDTH]
        kmean_ref[j] = jnp.mean(kb, axis=0, keepdims=True)


def _ffn2_out_kernel(h_ref, g2_ref, wg_ref, wu_ref, wd_ref, gfin_ref, o_ref):
    h = _swiglu_residual(h_ref[...], g2_ref[...], wg_ref, wu_ref, wd_ref)
    o_ref[...] = _rmsnorm(h, gfin_ref[...])


def _moba_kernel(q_ref, k_ref, v_ref, kmean_ref, o_ref, vta_ref, qta_ref, s_ref, p_ref, acc_ref):
    qi = pl.program_id(1)
    seq = k_ref.shape[1]
    n_blocks = seq // MOBA_BLOCK
    scale = A_HEAD_DIM ** -0.5

    @pl.when(qi == 0)
    def _():
        ones_row = (lax.broadcasted_iota(jnp.int32, (BF16_ROWS, seq), 0) == 0).astype(BF16)
        for h in range(A_HEADS):
            vta_ref[h * V_ROWS + A_HEAD_DIM:(h + 1) * V_ROWS, :] = ones_row
            qta_ref[h, LANES + n_blocks:, :] = jnp.zeros((LANES - n_blocks, MOBA_BLOCK), BF16)
        for n in range(n_blocks):
            blk = slice(n * MOBA_BLOCK, (n + 1) * MOBA_BLOCK)
            vt = v_ref[0, blk, :].astype(F32).T.astype(BF16)
            for h in range(A_HEADS):
                vta_ref[h * V_ROWS:h * V_ROWS + A_HEAD_DIM, blk] = vt[h * A_HEAD_DIM:(h + 1) * A_HEAD_DIM, :]

    lane = lax.broadcasted_iota(jnp.int32, (MOBA_BLOCK, LANES), 1)
    feat = lax.broadcasted_iota(jnp.int32, (LANES, MOBA_BLOCK), 0)
    blk_row = lax.broadcasted_iota(jnp.int32, (n_blocks, MOBA_BLOCK), 0)
    key_pos = lax.broadcasted_iota(jnp.int32, (MOBA_BLOCK, MOBA_BLOCK), 0)
    qry_pos = lax.broadcasted_iota(jnp.int32, (MOBA_BLOCK, MOBA_BLOCK), 1)
    valid = blk_row < qi

    for pair in range(HEAD_PAIRS):
        cols = slice(pair * LANES, (pair + 1) * LANES)
        qt = (q_ref[0, :, cols].astype(F32) * scale).T
        kmean = kmean_ref[0, :, cols].astype(BF16)
        for half in range(2):
            h = 2 * pair + half
            in_head = (feat >= half * A_HEAD_DIM) & (feat < (half + 1) * A_HEAD_DIM)
            qth = jnp.where(in_head, qt, 0.0).astype(BF16)
            gate = jnp.where(valid, _dot(kmean, qth), -jnp.inf)
            rank = jnp.zeros(gate.shape, jnp.int32)
            for m in range(n_blocks):
                gm = gate[m:m + 1, :]
                ahead = (gm > gate) | ((gm == gate) & (m < blk_row))
                rank = rank + jnp.where(ahead, 1, 0)
            open_block = (valid & (rank < MOBA_TOPK)) | (blk_row == qi)
            qta_ref[h, :LANES, :] = qth
            qta_ref[h, LANES:LANES + n_blocks, :] = jnp.where(open_block, 0.0, MASKED).astype(BF16)

    def step(n, m_run):
        own = m_run is None
        start = pl.multiple_of(n * MOBA_BLOCK, MOBA_BLOCK)
        block_onehot = (lane == n).astype(BF16)
        for pair in range(HEAD_PAIRS):
            cols = slice(pair * LANES, (pair + 1) * LANES)
            k_aug = jnp.concatenate([k_ref[0, pl.ds(start, MOBA_BLOCK), cols], block_onehot], axis=1)
            for h in (2 * pair, 2 * pair + 1):
                s = _dot(k_aug, qta_ref[h])
                s_ref[h] = jnp.where(key_pos <= qry_pos, s, MASKED) if own else s
        m_out, alphas = [], []
        for h in range(A_HEADS):
            s = s_ref[h]
            m_new = jnp.max(s, axis=0, keepdims=True)
            if not own:
                m_new = jnp.maximum(m_run[h], m_new)
                alphas.append(jnp.exp(m_run[h] - m_new))
            p_ref[h] = jnp.exp(s - m_new).astype(BF16)
            m_out.append(m_new)
        for h in range(A_HEADS):
            pv = _dot(vta_ref[h * V_ROWS:(h + 1) * V_ROWS, pl.ds(start, MOBA_BLOCK)], p_ref[h])
            acc_ref[h] = pv if own else alphas[h] * acc_ref[h] + pv
        return tuple(m_out)

    lax.fori_loop(0, qi, step, step(qi, None))

    for pair in range(HEAD_PAIRS):
        outs = []
        for h in (2 * pair, 2 * pair + 1):
            acc = acc_ref[h]
            outs.append(acc[:A_HEAD_DIM] / acc[A_HEAD_DIM:A_HEAD_DIM + 1])
        o_ref[0, :, pair * LANES:(pair + 1) * LANES] = jnp.concatenate(outs, axis=0).T.astype(o_ref.dtype)


def _mix_kernel(h_ref, attn_ref, gmix_ref, wuv_ref, lng_ref, lnb_ref, ws_ref, bs_ref,
                wba_ref, wbg_ref, wgate_ref, bgate_ref, wout_ref, o_ref):
    h = h_ref[...]
    n = _rmsnorm(h, gmix_ref[...]).astype(BF16)
    tm = h.shape[0]
    n_chunks = tm // G_CHUNK

    zg = jax.nn.gelu(_dot(n, wuv_ref[...]), approximate=True)
    u = zg[:, :G_WIDTH]
    v = zg[:, G_WIDTH:]
    mu = jnp.mean(v, axis=-1, keepdims=True)
    var = jnp.mean(jnp.square(v - mu), axis=-1, keepdims=True)
    v = ((v - mu) * lax.rsqrt(var + EPS) * lng_ref[...] + lnb_ref[...]).astype(BF16)

    row = lax.broadcasted_iota(jnp.int32, (G_CHUNK, G_CHUNK), 0)
    col = lax.broadcasted_iota(jnp.int32, (G_CHUNK, G_CHUNK), 1)
    lane = lax.broadcasted_iota(jnp.int32, (G_CHUNK, n_chunks * LANES), 1)
    low_half = (lane % LANES) < G_GROUP_DIM
    mixed_cols = []
    for p in range(GROUP_PAIRS):
        vp = jnp.concatenate(
            [v[c * G_CHUNK:(c + 1) * G_CHUNK, p * LANES:(p + 1) * LANES] for c in range(n_chunks)], axis=1)
        zeros = jnp.zeros_like(vp)
        rhs = jnp.concatenate([jnp.where(low_half, vp, zeros), jnp.where(low_half, zeros, vp)], axis=0)
        w_lo = jnp.where(col <= row, ws_ref[2 * p], 0.0).astype(BF16)
        w_hi = jnp.where(col <= row, ws_ref[2 * p + 1], 0.0).astype(BF16)
        mp = _dot(jnp.concatenate([w_lo, w_hi], axis=1), rhs)
        mixed_cols.append(jnp.concatenate(
            [mp[:, c * LANES:(c + 1) * LANES] for c in range(n_chunks)], axis=0))
    mixed = jnp.concatenate(mixed_cols, axis=1)
    bias = jnp.concatenate([bs_ref[...]] * n_chunks, axis=0)
    gm = (u * (mixed + bias)).astype(BF16)

    y_attn = _dot(attn_ref[...], wba_ref[...])
    y_gmlp = _dot(gm, wbg_ref[...])
    gates = jax.nn.sigmoid(_dot(n, wgate_ref[...]) + bgate_ref[...])
    merged = gates[:, :D_MODEL] * y_attn + gates[:, D_MODEL:] * y_gmlp
    o_ref[...] = h + _dot(merged.astype(BF16), wout_ref[...])


def _resident(shape):
    zeros = (0,) * len(shape)
    return pl.BlockSpec(shape, lambda *_: zeros, pipeline_mode=pl.Buffered(1))


def _row_tiles(width):
    return pl.BlockSpec((TOKEN_TILE, width), lambda i: (i, 0))


def _dense_params():
    return pltpu.CompilerParams(dimension_semantics=("parallel",), vmem_limit_bytes=VMEM_LIMIT_BYTES)


def kernel(x, ffn1_norm, ffn1_w_gate, ffn1_w_up, ffn1_w_down, mix_norm, w_in, gmlp_ln_g, gmlp_ln_b, gmlp_w_s, gmlp_b_s, w_branch_attn, w_branch_gmlp, w_gate, b_gate, w_out, ffn2_norm, ffn2_w_gate, ffn2_w_up, ffn2_w_down, final_norm):
    B, S, D = x.shape
    T = B * S
    n_blocks = S // MOBA_BLOCK
    assert D == D_MODEL and S % MOBA_BLOCK == 0 and T % TOKEN_TILE == 0
    assert TOKEN_TILE % MOBA_BLOCK == 0 and TOKEN_TILE % G_CHUNK == 0
    assert n_blocks <= LANES, "one one-hot lane per key block"
    assert ffn1_norm.shape[0] == 1, "single-layer block"
    bf = lambda w: w[0].astype(BF16)
    vec = lambda w: w.reshape(1, -1).astype(F32)
    n_tiles = T // TOKEN_TILE
    x2 = x.reshape(T, D)

    h1, qkv, kmean = pl.pallas_call(
        _ffn1_qkv_kernel,
        grid=(n_tiles,),
        in_specs=[_row_tiles(D), _resident((1, D)), _resident((D, D_FF)), _resident((D, D_FF)),
                  _resident((D_FF, D)), _resident((1, D)), _resident((D, 3 * A_WIDTH))],
        out_specs=[_row_tiles(D), _row_tiles(3 * A_WIDTH),
                   pl.BlockSpec((TOKEN_TILE // MOBA_BLOCK, 1, A_WIDTH), lambda i: (i, 0, 0))],
        out_shape=[jax.ShapeDtypeStruct((T, D), F32), jax.ShapeDtypeStruct((T, 3 * A_WIDTH), BF16),
                   jax.ShapeDtypeStruct((T // MOBA_BLOCK, 1, A_WIDTH), F32)],
        compiler_params=_dense_params(),
        name="ffn1_qkv",
    )(x2, vec(ffn1_norm), bf(ffn1_w_gate), bf(ffn1_w_up), bf(ffn1_w_down), vec(mix_norm),
      bf(w_in[:, :, :3 * A_WIDTH]))

    qkv3 = qkv.reshape(B, S, 3 * A_WIDTH)
    attn = pl.pallas_call(
        _moba_kernel,
        grid=(B, n_blocks),
        in_specs=[pl.BlockSpec((1, MOBA_BLOCK, A_WIDTH), lambda b, i: (b, i, 0)),
                  pl.BlockSpec((1, S, A_WIDTH), lambda b, i: (b, 0, 1)),
                  pl.BlockSpec((1, S, A_WIDTH), lambda b, i: (b, 0, 2)),
                  pl.BlockSpec((1, n_blocks, A_WIDTH), lambda b, i: (b, 0, 0))],
        out_specs=pl.BlockSpec((1, MOBA_BLOCK, A_WIDTH), lambda b, i: (b, i, 0)),
        out_shape=jax.ShapeDtypeStruct((B, S, A_WIDTH), BF16),
        scratch_shapes=[pltpu.VMEM((A_HEADS * V_ROWS, S), BF16),
                        pltpu.VMEM((A_HEADS, 2 * LANES, MOBA_BLOCK), BF16),
                        pltpu.VMEM((A_HEADS, MOBA_BLOCK, MOBA_BLOCK), F32),
                        pltpu.VMEM((A_HEADS, MOBA_BLOCK, MOBA_BLOCK), BF16),
                        pltpu.VMEM((A_HEADS, V_ROWS, MOBA_BLOCK), F32)],
        compiler_params=pltpu.CompilerParams(dimension_semantics=("parallel", "arbitrary"),
                                             vmem_limit_bytes=VMEM_LIMIT_BYTES),
        name="moba_attn",
    )(qkv3, qkv3, qkv3, kmean.reshape(B, n_blocks, A_WIDTH))

    bs_lanes = jnp.repeat(gmlp_b_s[0].T, G_GROUP_DIM, axis=1).astype(F32)
    h2 = pl.pallas_call(
        _mix_kernel,
        grid=(n_tiles,),
        in_specs=[_row_tiles(D), _row_tiles(A_WIDTH), _resident((1, D)), _resident((D, 2 * G_WIDTH)),
                  _resident((1, G_WIDTH)), _resident((1, G_WIDTH)),
                  _resident((G_GROUPS, G_CHUNK, G_CHUNK)), _resident((G_CHUNK, G_WIDTH)),
                  _resident((A_WIDTH, D)), _resident((G_WIDTH, D)), _resident((D, 2 * D)),
                  _resident((1, 2 * D)), _resident((D, D))],
        out_specs=_row_tiles(D),
        out_shape=jax.ShapeDtypeStruct((T, D), F32),
        compiler_params=_dense_params(),
        name="mix",
    )(h1, attn.reshape(T, A_WIDTH), vec(mix_norm), bf(w_in[:, :, 3 * A_WIDTH:]), vec(gmlp_ln_g),
      vec(gmlp_ln_b), gmlp_w_s[0].astype(F32), bs_lanes, bf(w_branch_attn), bf(w_branch_gmlp),
      bf(w_gate), vec(b_gate), bf(w_out))

    out = pl.pallas_call(
        _ffn2_out_kernel,
        grid=(n_tiles,),
        in_specs=[_row_tiles(D), _resident((1, D)), _resident((D, D_FF)), _resident((D, D_FF)),
                  _resident((D_FF, D)), _resident((1, D))],
        out_specs=_row_tiles(D),
        out_shape=jax.ShapeDtypeStruct((T, D), F32),
        compiler_params=_dense_params(),
        name="ffn2_out",
    )(h2, vec(ffn2_norm), bf(ffn2_w_gate), bf(ffn2_w_up), bf(ffn2_w_down), vec(final_norm))
    return out.reshape(B, S, D)
```

```python
import jax
import jax.numpy as jnp
from jax import lax
from jax.experimental import pallas as pl
from jax.experimental.pallas import tpu as pltpu

D_MODEL = 1024
D_FF = 2816
A_HEADS = 8
A_HEAD_DIM = 64
A_WIDTH = A_HEADS * A_HEAD_DIM
MOBA_BLOCK = 256
MOBA_TOPK = 3
G_GROUPS = 8
G_CHUNK = 128
G_WIDTH = 512
G_GROUP_DIM = G_WIDTH // G_GROUPS
EPS = 1e-6

LANES = 128
BF16_ROWS = 16
V_ROWS = A_HEAD_DIM + BF16_ROWS
HEAD_PAIRS = A_WIDTH // LANES
GROUP_PAIRS = G_WIDTH // LANES
MXU_COLS = 256
FF_CHUNKS = (768, 768, 768, 512)
assert sum(FF_CHUNKS) == D_FF and all(c % MXU_COLS == 0 for c in FF_CHUNKS)

TOKEN_TILE = 512
SIDE_CAST_BLOCKS = 16
FIRST_CAST_BLOCKS = 8
VMEM_LIMIT_BYTES = 56 * 1024 * 1024
MASKED = -1e30
LOG2_E = 1.4426950408889634
LONG_RUN = 4
PV_LAG = 1
F32 = jnp.float32
BF16 = jnp.bfloat16


def _dot(a, b):
    return jnp.dot(a, b, preferred_element_type=F32)


def _rmsnorm(x, g):
    return x * lax.rsqrt(jnp.mean(x * x, axis=-1, keepdims=True) + EPS) * g


def _swiglu_residual(x, norm_g, wg_ref, wu_ref, wd_ref):
    n = _rmsnorm(x, norm_g).astype(BF16)
    acc = jnp.zeros(x.shape, F32)
    start = 0
    for width in FF_CHUNKS:
        gate = _dot(n, wg_ref[:, start:start + width])
        up = _dot(n, wu_ref[:, start:start + width])
        act = (gate * jax.nn.sigmoid(gate) * up).astype(BF16)
        acc = acc + _dot(act, wd_ref[start:start + width, :])
        start += width
    return x + 0.5 * acc


def _ffn1_qkv_kernel(x_ref, g1_ref, wg_ref, wu_ref, wd_ref, gmix_ref, wqkv_ref, *rest):
    k = (len(rest) - 4) // 2
    side_in, (h_ref, qkv_ref, kmean_ref), side_out = rest[:k + 1], rest[k + 1:k + 4], rest[k + 4:]
    for src, dst in zip(side_in[:-2], side_out[:-1]):
        dst[...] = src[...].astype(BF16)
    half = side_in[-1].shape[-1]
    side_out[-1][:, :half] = side_in[-2][...].astype(BF16)
    side_out[-1][:, half:] = side_in[-1][...].astype(BF16)
    h = _swiglu_residual(x_ref[...], g1_ref[...], wg_ref, wu_ref, wd_ref)
    h_ref[...] = h
    n = _rmsnorm(h, gmix_ref[...]).astype(BF16)
    z = _dot(n, wqkv_ref[...])
    qkv_ref[...] = z.astype(BF16)
    for j in range(TOKEN_TILE // MOBA_BLOCK):
        kb = z[j * MOBA_BLOCK:(j + 1) * MOBA_BLOCK, A_WIDTH:2 * A_WIDTH]
        kmean_ref[j] = jnp.mean(kb, axis=0, keepdims=True)


def _cast_kernel(*refs):
    n = len(refs) // 2
    for src, dst in zip(refs[:n], refs[n:]):
        dst[...] = src[...].astype(BF16)


def _ffn2_out_kernel(h_ref, g2_ref, wg_ref, wu_ref, wd_ref, gfin_ref, o_ref):
    h = _swiglu_residual(h_ref[...], g2_ref[...], wg_ref, wu_ref, wd_ref)
    o_ref[...] = _rmsnorm(h, gfin_ref[...])


def _moba_kernel(q_ref, k_ref, v_ref, kmean_ref, o_ref, vta_ref, qta_ref, s_ref, smax_ref, p_ref, acc_ref):
    qi = pl.program_id(1)
    seq = k_ref.shape[1]
    n_blocks = seq // MOBA_BLOCK
    scale = A_HEAD_DIM ** -0.5

    @pl.when(qi == 0)
    def _():
        ones_row = (lax.broadcasted_iota(jnp.int32, (BF16_ROWS, seq), 0) == 0).astype(BF16)
        for h in range(A_HEADS):
            vta_ref[h * V_ROWS + A_HEAD_DIM:(h + 1) * V_ROWS, :] = ones_row
            qta_ref[h, LANES + n_blocks:, :] = jnp.zeros((LANES - n_blocks, MOBA_BLOCK), BF16)
        for n in range(n_blocks):
            blk = slice(n * MOBA_BLOCK, (n + 1) * MOBA_BLOCK)
            vt = v_ref[0, blk, :].astype(F32).T.astype(BF16)
            for h in range(A_HEADS):
                vta_ref[h * V_ROWS:h * V_ROWS + A_HEAD_DIM, blk] = vt[h * A_HEAD_DIM:(h + 1) * A_HEAD_DIM, :]

    lane = lax.broadcasted_iota(jnp.int32, (MOBA_BLOCK, LANES), 1)
    feat = lax.broadcasted_iota(jnp.int32, (LANES, MOBA_BLOCK), 0)
    blk_row = lax.broadcasted_iota(jnp.int32, (n_blocks, MOBA_BLOCK), 0)
    key_pos = lax.broadcasted_iota(jnp.int32, (MOBA_BLOCK, MOBA_BLOCK), 0)
    qry_pos = lax.broadcasted_iota(jnp.int32, (MOBA_BLOCK, MOBA_BLOCK), 1)
    valid = blk_row < qi

    def score_head(n, slot, h, own=False):
        start = pl.multiple_of(n * MOBA_BLOCK, MOBA_BLOCK)
        k_pair = k_ref[0, pl.ds(start, MOBA_BLOCK), (h // 2) * LANES:(h // 2 + 1) * LANES]
        if own:
            s = jnp.where(key_pos <= qry_pos, _dot(k_pair, qta_ref[h, :LANES, :]), MASKED)
        else:
            s = _dot(jnp.concatenate([k_pair, (lane == n).astype(BF16)], axis=1), qta_ref[h])
        s_ref[slot, h] = s
        smax_ref[slot, h] = jnp.max(s, axis=0, keepdims=True)

    for pair in range(HEAD_PAIRS):
        cols = slice(pair * LANES, (pair + 1) * LANES)
        qt = (q_ref[0, :, cols].astype(F32) * scale).T
        kmean = kmean_ref[0, :, cols].astype(BF16)
        for half in range(2):
            h = 2 * pair + half
            in_head = (feat >= half * A_HEAD_DIM) & (feat < (half + 1) * A_HEAD_DIM)
            qth = jnp.where(in_head, qt, 0.0).astype(BF16)
            qta_ref[h, :LANES, :] = jnp.where(in_head, qt * LOG2_E, 0.0).astype(BF16)
            score_head(qi, 0, h, own=True)
            gate = jnp.where(valid, _dot(kmean, qth), -jnp.inf)
            selected = jnp.zeros(gate.shape, jnp.bool_)
            for _ in range(MOBA_TOPK):
                best = jnp.max(gate, axis=0, keepdims=True)
                first = jnp.min(jnp.where(gate == best, blk_row, n_blocks), axis=0, keepdims=True)
                pick = blk_row == first
                selected = selected | pick
                gate = jnp.where(pick, -jnp.inf, gate)
            selected = selected & valid
            qta_ref[h, LANES:LANES + n_blocks, :] = jnp.where(selected, 0.0, MASKED).astype(BF16)

    def softmax_head(slot, h, m_prev):
        s = s_ref[slot, h]
        m_new = smax_ref[slot, h]
        alpha = None
        if m_prev is not None:
            m_new = jnp.maximum(m_prev, m_new)
            alpha = jnp.exp2(m_prev - m_new)
        p_ref[h] = jnp.exp2(s - m_new).astype(BF16)
        return m_new, alpha

    def value_head(n, h, alpha):
        start = pl.multiple_of(n * MOBA_BLOCK, MOBA_BLOCK)
        pv = _dot(vta_ref[h * V_ROWS:(h + 1) * V_ROWS, pl.ds(start, MOBA_BLOCK)], p_ref[h])
        acc_ref[h] = pv if alpha is None else alpha * acc_ref[h] + pv

    def advance(blocks, m_run):
        m_run = None if m_run is None else list(m_run)
        waiting = []
        for n, slot, n_next in blocks:
            m_out = []
            for h in range(A_HEADS):
                if n_next is not None:
                    score_head(n_next, 1 - slot, h)
                m_new, alpha = softmax_head(slot, h, None if m_run is None else m_run[h])
                m_out.append(m_new)
                waiting.append((n, h, alpha))
                if len(waiting) > PV_LAG:
                    value_head(*waiting.pop(0))
            m_run = m_out
        for args in waiting:
            value_head(*args)
        return tuple(m_run)

    m_run = advance([(qi, 0, 0)], None)

    def run_of_blocks(n0, count, m_run):
        return advance([(n0 + i, (i + 1) % 2, n0 + i + 1) for i in range(count)], m_run)

    ahead = jnp.maximum(qi - 1, 0)
    long_trips = lax.shift_right_logical(ahead, LONG_RUN.bit_length() - 1)
    pairs_start = long_trips * LONG_RUN
    pair_trips = lax.shift_right_logical(ahead - pairs_start, 1)
    single_start = pairs_start + 2 * pair_trips
    m_run = lax.fori_loop(0, long_trips, lambda j, m: run_of_blocks(j * LONG_RUN, LONG_RUN, m), m_run)
    m_run = lax.fori_loop(0, pair_trips, lambda j, m: run_of_blocks(pairs_start + 2 * j, 2, m), m_run)
    m_run = lax.fori_loop(0, ahead - single_start, lambda j, m: run_of_blocks(single_start, 1, m), m_run)
    last_is_even = (ahead & 1) == 0
    m_run = lax.fori_loop(0, jnp.where((qi > 0) & last_is_even, 1, 0),
                          lambda j, m: advance([(ahead, 1, None)], m), m_run)
    lax.fori_loop(0, jnp.where((qi > 0) & ~last_is_even, 1, 0),
                  lambda j, m: advance([(ahead, 0, None)], m), m_run)

    for pair in range(HEAD_PAIRS):
        outs = []
        for h in (2 * pair, 2 * pair + 1):
            acc = acc_ref[h]
            outs.append(acc[:A_HEAD_DIM] / acc[A_HEAD_DIM:A_HEAD_DIM + 1])
        o_ref[0, :, pair * LANES:(pair + 1) * LANES] = jnp.concatenate(outs, axis=0).T.astype(o_ref.dtype)


def _mix_kernel(h_ref, attn_ref, gmix_ref, wuv_ref, lng_ref, lnb_ref, ws_ref, bs_ref,
                wba_ref, wbg_ref, wgate_ref, bgate_ref, wout_ref, o_ref):
    h = h_ref[...]
    n = _rmsnorm(h, gmix_ref[...]).astype(BF16)
    tm = h.shape[0]
    n_chunks = tm // G_CHUNK

    zg = jax.nn.gelu(_dot(n, wuv_ref[...]), approximate=True)
    u = zg[:, :G_WIDTH]
    v = zg[:, G_WIDTH:]
    mu = jnp.mean(v, axis=-1, keepdims=True)
    var = jnp.mean(jnp.square(v - mu), axis=-1, keepdims=True)
    v = ((v - mu) * lax.rsqrt(var + EPS) * lng_ref[...] + lnb_ref[...]).astype(BF16)

    row = lax.broadcasted_iota(jnp.int32, (G_CHUNK, G_CHUNK), 0)
    col = lax.broadcasted_iota(jnp.int32, (G_CHUNK, G_CHUNK), 1)
    lane = lax.broadcasted_iota(jnp.int32, (G_CHUNK, n_chunks * LANES), 1)
    low_half = (lane % LANES) < G_GROUP_DIM
    mixed_cols = []
    for p in range(GROUP_PAIRS):
        vp = jnp.concatenate(
            [v[c * G_CHUNK:(c + 1) * G_CHUNK, p * LANES:(p + 1) * LANES] for c in range(n_chunks)], axis=1)
        zeros = jnp.zeros_like(vp)
        rhs = jnp.concatenate([jnp.where(low_half, vp, zeros), jnp.where(low_half, zeros, vp)], axis=0)
        w_lo = jnp.where(col <= row, ws_ref[2 * p], 0.0).astype(BF16)
        w_hi = jnp.where(col <= row, ws_ref[2 * p + 1], 0.0).astype(BF16)
        mp = _dot(jnp.concatenate([w_lo, w_hi], axis=1), rhs)
        mixed_cols.append(jnp.concatenate(
            [mp[:, c * LANES:(c + 1) * LANES] for c in range(n_chunks)], axis=0))
    mixed = jnp.concatenate(mixed_cols, axis=1)
    bias = jnp.concatenate([bs_ref[...]] * n_chunks, axis=0)
    gm = (u * (mixed + bias)).astype(BF16)

    y_attn = _dot(attn_ref[...], wba_ref[...])
    y_gmlp = _dot(gm, wbg_ref[...])
    gates = jax.nn.sigmoid(_dot(n, wgate_ref[...]) + bgate_ref[...])
    merged = gates[:, :D_MODEL] * y_attn + gates[:, D_MODEL:] * y_gmlp
    o_ref[...] = h + _dot(merged.astype(BF16), wout_ref[...])


def _resident(shape):
    zeros = (0,) * len(shape)
    return pl.BlockSpec(shape, lambda *_: zeros, pipeline_mode=pl.Buffered(1))


def _row_tiles(width):
    return pl.BlockSpec((TOKEN_TILE, width), lambda i: (i, 0))


def _dense_params():
    return pltpu.CompilerParams(dimension_semantics=("parallel",), vmem_limit_bytes=VMEM_LIMIT_BYTES)


def kernel(x, ffn1_norm, ffn1_w_gate, ffn1_w_up, ffn1_w_down, mix_norm, w_in, gmlp_ln_g, gmlp_ln_b, gmlp_w_s, gmlp_b_s, w_branch_attn, w_branch_gmlp, w_gate, b_gate, w_out, ffn2_norm, ffn2_w_gate, ffn2_w_up, ffn2_w_down, final_norm):
    B, S, D = x.shape
    T = B * S
    n_blocks = S // MOBA_BLOCK
    assert D == D_MODEL and S % MOBA_BLOCK == 0 and T % TOKEN_TILE == 0
    assert TOKEN_TILE % MOBA_BLOCK == 0 and TOKEN_TILE % G_CHUNK == 0
    assert n_blocks <= LANES, "one one-hot lane per key block"
    assert ffn1_norm.shape[0] == 1, "single-layer block"
    vec = lambda w: w.reshape(1, -1).astype(F32)
    n_tiles = T // TOKEN_TILE
    x2 = x.reshape(T, D)

    first = [(ffn1_w_gate, D_FF), (ffn1_w_up, D_FF), (ffn1_w_down, D), (w_in, 3 * A_WIDTH)]
    assert all(w.shape[1] % (FIRST_CAST_BLOCKS * BF16_ROWS) == 0 for w, _ in first)
    ffn1_wg, ffn1_wu, ffn1_wd, w_qkv = pl.pallas_call(
        _cast_kernel,
        grid=(FIRST_CAST_BLOCKS,),
        in_specs=[pl.BlockSpec((None, w.shape[1] // FIRST_CAST_BLOCKS, cols), lambda i: (0, i, 0))
                  for w, cols in first],
        out_specs=[pl.BlockSpec((w.shape[1] // FIRST_CAST_BLOCKS, cols), lambda i: (i, 0)) for w, cols in first],
        out_shape=[jax.ShapeDtypeStruct((w.shape[1], cols), BF16) for w, cols in first],
        compiler_params=_dense_params(),
        name="cast_first",
    )(*[w for w, _ in first])

    later = [ffn2_w_gate, ffn2_w_up, ffn2_w_down, w_gate, w_out, w_branch_attn, w_branch_gmlp]
    steps_per_block, rest = divmod(n_tiles, SIDE_CAST_BLOCKS)
    assert rest == 0 and all(w.shape[1] % (SIDE_CAST_BLOCKS * BF16_ROWS) == 0 for w in later + [w_in])
    row_block = lambda i: i // steps_per_block
    side_in_specs = [pl.BlockSpec((None, w.shape[1] // SIDE_CAST_BLOCKS, w.shape[2]), lambda i: (0, row_block(i), 0))
                     for w in later]
    side_out_specs = [pl.BlockSpec((w.shape[1] // SIDE_CAST_BLOCKS, w.shape[2]), lambda i: (row_block(i), 0))
                      for w in later]
    side_out_shapes = [jax.ShapeDtypeStruct(w.shape[1:], BF16) for w in later]
    assert (3 * A_WIDTH) % G_WIDTH == 0
    first_g = 3 * A_WIDTH // G_WIDTH
    side_in_specs += [pl.BlockSpec((None, D // SIDE_CAST_BLOCKS, G_WIDTH), lambda i, c=c: (0, row_block(i), first_g + c))
                      for c in range(2)]
    side_out_specs.append(pl.BlockSpec((D // SIDE_CAST_BLOCKS, 2 * G_WIDTH), lambda i: (row_block(i), 0)))
    side_out_shapes.append(jax.ShapeDtypeStruct((D, 2 * G_WIDTH), BF16))
    h1, qkv, kmean, *later_bf16 = pl.pallas_call(
        _ffn1_qkv_kernel,
        grid=(n_tiles,),
        in_specs=[_row_tiles(D), _resident((1, D)), _resident((D, D_FF)), _resident((D, D_FF)),
                  _resident((D_FF, D)), _resident((1, D)), _resident((D, 3 * A_WIDTH))] + side_in_specs,
        out_specs=[_row_tiles(D), _row_tiles(3 * A_WIDTH),
                   pl.BlockSpec((TOKEN_TILE // MOBA_BLOCK, 1, A_WIDTH), lambda i: (i, 0, 0))] + side_out_specs,
        out_shape=[jax.ShapeDtypeStruct((T, D), F32), jax.ShapeDtypeStruct((T, 3 * A_WIDTH), BF16),
                   jax.ShapeDtypeStruct((T // MOBA_BLOCK, 1, A_WIDTH), F32)] + side_out_shapes,
        compiler_params=pltpu.CompilerParams(dimension_semantics=("arbitrary",),
                                             vmem_limit_bytes=VMEM_LIMIT_BYTES),
        name="ffn1_qkv",
    )(x2, vec(ffn1_norm), ffn1_wg, ffn1_wu, ffn1_wd, vec(mix_norm), w_qkv, *later, w_in, w_in)
    ffn2_wg, ffn2_wu, ffn2_wd, w_gate_bf, w_out_bf, w_ba_bf, w_bg_bf, w_uv_bf = later_bf16

    qkv3 = qkv.reshape(B, S, 3 * A_WIDTH)
    attn = pl.pallas_call(
        _moba_kernel,
        grid=(B, n_blocks),
        in_specs=[pl.BlockSpec((1, MOBA_BLOCK, A_WIDTH), lambda b, i: (b, i, 0)),
                  pl.BlockSpec((1, S, A_WIDTH), lambda b, i: (b, 0, 1)),
                  pl.BlockSpec((1, S, A_WIDTH), lambda b, i: (b, 0, 2)),
                  pl.BlockSpec((1, n_blocks, A_WIDTH), lambda b, i: (b, 0, 0))],
        out_specs=pl.BlockSpec((1, MOBA_BLOCK, A_WIDTH), lambda b, i: (b, i, 0)),
        out_shape=jax.ShapeDtypeStruct((B, S, A_WIDTH), BF16),
        scratch_shapes=[pltpu.VMEM((A_HEADS * V_ROWS, S), BF16),
                        pltpu.VMEM((A_HEADS, 2 * LANES, MOBA_BLOCK), BF16),
                        pltpu.VMEM((2, A_HEADS, MOBA_BLOCK, MOBA_BLOCK), F32),
                        pltpu.VMEM((2, A_HEADS, 1, MOBA_BLOCK), F32),
                        pltpu.VMEM((A_HEADS, MOBA_BLOCK, MOBA_BLOCK), BF16),
                        pltpu.VMEM((A_HEADS, V_ROWS, MOBA_BLOCK), F32)],
        compiler_params=pltpu.CompilerParams(dimension_semantics=("parallel", "arbitrary"),
                                             vmem_limit_bytes=VMEM_LIMIT_BYTES),
        name="moba_attn",
    )(qkv3, qkv3, qkv3, kmean.reshape(B, n_blocks, A_WIDTH))

    bs_lanes = jnp.repeat(gmlp_b_s[0].T, G_GROUP_DIM, axis=1).astype(F32)
    h2 = pl.pallas_call(
        _mix_kernel,
        grid=(n_tiles,),
        in_specs=[_row_tiles(D), _row_tiles(A_WIDTH), _resident((1, D)), _resident((D, 2 * G_WIDTH)),
                  _resident((1, G_WIDTH)), _resident((1, G_WIDTH)),
                  _resident((G_GROUPS, G_CHUNK, G_CHUNK)), _resident((G_CHUNK, G_WIDTH)),
                  _resident((A_WIDTH, D)), _resident((G_WIDTH, D)), _resident((D, 2 * D)),
                  _resident((1, 2 * D)), _resident((D, D))],
        out_specs=_row_tiles(D),
        out_shape=jax.ShapeDtypeStruct((T, D), F32),
        compiler_params=_dense_params(),
        name="mix",
    )(h1, attn.reshape(T, A_WIDTH), vec(mix_norm), w_uv_bf, vec(gmlp_ln_g),
      vec(gmlp_ln_b), gmlp_w_s[0].astype(F32), bs_lanes, w_ba_bf, w_bg_bf,
      w_gate_bf, vec(b_gate), w_out_bf)

    out = pl.pallas_call(
        _ffn2_out_kernel,
        grid=(n_tiles,),
        in_specs=[_row_tiles(D), _resident((1, D)), _resident((D, D_FF)), _resident((D, D_FF)),
                  _resident((D_FF, D)), _resident((1, D))],
        out_specs=_row_tiles(D),
        out_shape=jax.ShapeDtypeStruct((T, D), F32),
        compiler_params=_dense_params(),
        name="ffn2_out",
    )(h2, vec(ffn2_norm), ffn2_wg, ffn2_wu, ffn2_wd, vec(final_norm))
    return out.reshape(B, S, D)
```

```python
import jax
import jax.numpy as jnp
from jax import lax
from jax.experimental import pallas as pl
from jax.experimental.pallas import tpu as pltpu

D_MODEL = 1024
D_FF = 2816
A_HEADS = 8
A_HEAD_DIM = 64
A_WIDTH = A_HEADS * A_HEAD_DIM
MOBA_BLOCK = 256
MOBA_TOPK = 3
G_GROUPS = 8
G_CHUNK = 128
G_WIDTH = 512
G_GROUP_DIM = G_WIDTH // G_GROUPS
EPS = 1e-6

LANES = 128
BF16_ROWS = 16
V_ROWS = A_HEAD_DIM + BF16_ROWS
HEAD_PAIRS = A_WIDTH // LANES
GROUP_PAIRS = G_WIDTH // LANES
MXU_COLS = 256
FF_CHUNKS = (768, 768, 768, 512)
assert sum(FF_CHUNKS) == D_FF and all(c % MXU_COLS == 0 for c in FF_CHUNKS)

TOKEN_TILE = 512
SIDE_CAST_BLOCKS = 16
FIRST_CAST_BLOCKS = 8
VMEM_LIMIT_BYTES = 56 * 1024 * 1024
MASKED = -1e30
LOG2_E = 1.4426950408889634
LONG_RUN = 4
F32 = jnp.float32
BF16 = jnp.bfloat16


def _dot(a, b):
    return jnp.dot(a, b, preferred_element_type=F32)


def _rmsnorm(x, g):
    return x * lax.rsqrt(jnp.mean(x * x, axis=-1, keepdims=True) + EPS) * g


def _swiglu_residual(x, norm_g, wg_ref, wu_ref, wd_ref):
    n = _rmsnorm(x, norm_g).astype(BF16)
    acc = jnp.zeros(x.shape, F32)
    start = 0
    for width in FF_CHUNKS:
        gate = _dot(n, wg_ref[:, start:start + width])
        up = _dot(n, wu_ref[:, start:start + width])
        act = (gate * jax.nn.sigmoid(gate) * up).astype(BF16)
        acc = acc + _dot(act, wd_ref[start:start + width, :])
        start += width
    return x + 0.5 * acc


def _ffn1_qkv_kernel(x_ref, g1_ref, wg_ref, wu_ref, wd_ref, gmix_ref, wqkv_ref, *rest):
    k = (len(rest) - 4) // 2
    side_in, (h_ref, qkv_ref, kmean_ref), side_out = rest[:k + 1], rest[k + 1:k + 4], rest[k + 4:]
    for src, dst in zip(side_in[:-2], side_out[:-1]):
        dst[...] = src[...].astype(BF16)
    half = side_in[-1].shape[-1]
    side_out[-1][:, :half] = side_in[-2][...].astype(BF16)
    side_out[-1][:, half:] = side_in[-1][...].astype(BF16)
    h = _swiglu_residual(x_ref[...], g1_ref[...], wg_ref, wu_ref, wd_ref)
    h_ref[...] = h
    n = _rmsnorm(h, gmix_ref[...]).astype(BF16)
    z = _dot(n, wqkv_ref[...])
    qkv_ref[...] = z.astype(BF16)
    for j in range(TOKEN_TILE // MOBA_BLOCK):
        kb = z[j * MOBA_BLOCK:(j + 1) * MOBA_BLOCK, A_WIDTH:2 * A_WIDTH]
        kmean_ref[j] = jnp.mean(kb, axis=0, keepdims=True)


def _cast_kernel(*refs):
    n = len(refs) // 2
    for src, dst in zip(refs[:n], refs[n:]):
        dst[...] = src[...].astype(BF16)


def _ffn2_out_kernel(h_ref, g2_ref, wg_ref, wu_ref, wd_ref, gfin_ref, o_ref):
    h = _swiglu_residual(h_ref[...], g2_ref[...], wg_ref, wu_ref, wd_ref)
    o_ref[...] = _rmsnorm(h, gfin_ref[...])


def _moba_kernel(q_ref, k_ref, v_ref, kmean_ref, o_ref, vta_ref, qta_ref, s_ref, smax_ref, acc_ref):
    qi = pl.program_id(1)
    seq = k_ref.shape[1]
    n_blocks = seq // MOBA_BLOCK
    scale = A_HEAD_DIM ** -0.5

    @pl.when(qi == 0)
    def _():
        ones_row = (lax.broadcasted_iota(jnp.int32, (BF16_ROWS, seq), 0) == 0).astype(BF16)
        for h in range(A_HEADS):
            vta_ref[h * V_ROWS + A_HEAD_DIM:(h + 1) * V_ROWS, :] = ones_row
            qta_ref[h, LANES + n_blocks:, :] = jnp.zeros((LANES - n_blocks, MOBA_BLOCK), BF16)
        for n in range(n_blocks):
            blk = slice(n * MOBA_BLOCK, (n + 1) * MOBA_BLOCK)
            vt = v_ref[0, blk, :].astype(F32).T.astype(BF16)
            for h in range(A_HEADS):
                vta_ref[h * V_ROWS:h * V_ROWS + A_HEAD_DIM, blk] = vt[h * A_HEAD_DIM:(h + 1) * A_HEAD_DIM, :]

    lane = lax.broadcasted_iota(jnp.int32, (MOBA_BLOCK, LANES), 1)
    feat = lax.broadcasted_iota(jnp.int32, (LANES, MOBA_BLOCK), 0)
    blk_row = lax.broadcasted_iota(jnp.int32, (n_blocks, MOBA_BLOCK), 0)
    key_pos = lax.broadcasted_iota(jnp.int32, (MOBA_BLOCK, MOBA_BLOCK), 0)
    qry_pos = lax.broadcasted_iota(jnp.int32, (MOBA_BLOCK, MOBA_BLOCK), 1)
    valid = blk_row < qi

    def score_head(n, slot, h, own=False):
        start = pl.multiple_of(n * MOBA_BLOCK, MOBA_BLOCK)
        k_pair = k_ref[0, pl.ds(start, MOBA_BLOCK), (h // 2) * LANES:(h // 2 + 1) * LANES]
        if own:
            s = jnp.where(key_pos <= qry_pos, _dot(k_pair, qta_ref[h, :LANES, :]), MASKED)
        else:
            s = _dot(jnp.concatenate([k_pair, (lane == n).astype(BF16)], axis=1), qta_ref[h])
        s_ref[slot, h] = s
        smax_ref[slot, h] = jnp.max(s, axis=0, keepdims=True)

    for pair in range(HEAD_PAIRS):
        cols = slice(pair * LANES, (pair + 1) * LANES)
        qt = (q_ref[0, :, cols].astype(F32) * scale).T
        kmean = kmean_ref[0, :, cols].astype(BF16)
        for half in range(2):
            h = 2 * pair + half
            in_head = (feat >= half * A_HEAD_DIM) & (feat < (half + 1) * A_HEAD_DIM)
            qth = jnp.where(in_head, qt, 0.0).astype(BF16)
            qta_ref[h, :LANES, :] = jnp.where(in_head, qt * LOG2_E, 0.0).astype(BF16)
            score_head(qi, 0, h, own=True)
            gate = jnp.where(valid, _dot(kmean, qth), -jnp.inf)
            selected = jnp.zeros(gate.shape, jnp.bool_)
            for _ in range(MOBA_TOPK):
                best = jnp.max(gate, axis=0, keepdims=True)
                first = jnp.min(jnp.where(gate == best, blk_row, n_blocks), axis=0, keepdims=True)
                pick = blk_row == first
                selected = selected | pick
                gate = jnp.where(pick, -jnp.inf, gate)
            selected = selected & valid
            qta_ref[h, LANES:LANES + n_blocks, :] = jnp.where(selected, 0.0, MASKED).astype(BF16)

    def update_head(n, slot, h, m_prev):
        start = pl.multiple_of(n * MOBA_BLOCK, MOBA_BLOCK)
        m_new = smax_ref[slot, h]
        if m_prev is not None:
            m_new = jnp.maximum(m_prev, m_new)
        p = jnp.exp2(s_ref[slot, h] - m_new).astype(BF16)
        pv = _dot(vta_ref[h * V_ROWS:(h + 1) * V_ROWS, pl.ds(start, MOBA_BLOCK)], p)
        if m_prev is None:
            acc_ref[h] = pv
        else:
            acc_ref[h] = jnp.exp2(m_prev - m_new) * acc_ref[h] + pv
        return m_new

    def advance(blocks, m_run):
        for n, slot, n_next in blocks:
            m_out = []
            for h in range(A_HEADS):
                if n_next is not None:
                    score_head(n_next, 1 - slot, h)
                m_out.append(update_head(n, slot, h, None if m_run is None else m_run[h]))
            m_run = m_out
        return tuple(m_run)

    m_run = advance([(qi, 0, 0)], None)

    def run_of_blocks(n0, count, m_run):
        return advance([(n0 + i, (i + 1) % 2, n0 + i + 1) for i in range(count)], m_run)

    ahead = jnp.maximum(qi - 1, 0)
    long_trips = lax.shift_right_logical(ahead, LONG_RUN.bit_length() - 1)
    pairs_start = long_trips * LONG_RUN
    pair_trips = lax.shift_right_logical(ahead - pairs_start, 1)
    single_start = pairs_start + 2 * pair_trips
    m_run = lax.fori_loop(0, long_trips, lambda j, m: run_of_blocks(j * LONG_RUN, LONG_RUN, m), m_run)
    m_run = lax.fori_loop(0, pair_trips, lambda j, m: run_of_blocks(pairs_start + 2 * j, 2, m), m_run)
    m_run = lax.fori_loop(0, ahead - single_start, lambda j, m: run_of_blocks(single_start, 1, m), m_run)
    last_is_even = (ahead & 1) == 0
    m_run = lax.fori_loop(0, jnp.where((qi > 0) & last_is_even, 1, 0),
                          lambda j, m: advance([(ahead, 1, None)], m), m_run)
    lax.fori_loop(0, jnp.where((qi > 0) & ~last_is_even, 1, 0),
                  lambda j, m: advance([(ahead, 0, None)], m), m_run)

    for pair in range(HEAD_PAIRS):
        outs = []
        for h in (2 * pair, 2 * pair + 1):
            acc = acc_ref[h]
            outs.append(acc[:A_HEAD_DIM] / acc[A_HEAD_DIM:A_HEAD_DIM + 1])
        o_ref[0, :, pair * LANES:(pair + 1) * LANES] = jnp.concatenate(outs, axis=0).T.astype(o_ref.dtype)


def _mix_kernel(h_ref, attn_ref, gmix_ref, wuv_ref, lng_ref, lnb_ref, ws_ref, bs_ref,
                wba_ref, wbg_ref, wgate_ref, bgate_ref, wout_ref, o_ref):
    h = h_ref[...]
    n = _rmsnorm(h, gmix_ref[...]).astype(BF16)
    tm = h.shape[0]
    n_chunks = tm // G_CHUNK

    zg = jax.nn.gelu(_dot(n, wuv_ref[...]), approximate=True)
    u = zg[:, :G_WIDTH]
    v = zg[:, G_WIDTH:]
    mu = jnp.mean(v, axis=-1, keepdims=True)
    var = jnp.mean(jnp.square(v - mu), axis=-1, keepdims=True)
    v = ((v - mu) * lax.rsqrt(var + EPS) * lng_ref[...] + lnb_ref[...]).astype(BF16)

    row = lax.broadcasted_iota(jnp.int32, (G_CHUNK, G_CHUNK), 0)
    col = lax.broadcasted_iota(jnp.int32, (G_CHUNK, G_CHUNK), 1)
    lane = lax.broadcasted_iota(jnp.int32, (G_CHUNK, n_chunks * LANES), 1)
    low_half = (lane % LANES) < G_GROUP_DIM
    mixed_cols = []
    for p in range(GROUP_PAIRS):
        vp = jnp.concatenate(
            [v[c * G_CHUNK:(c + 1) * G_CHUNK, p * LANES:(p + 1) * LANES] for c in range(n_chunks)], axis=1)
        zeros = jnp.zeros_like(vp)
        rhs = jnp.concatenate([jnp.where(low_half, vp, zeros), jnp.where(low_half, zeros, vp)], axis=0)
        w_lo = jnp.where(col <= row, ws_ref[2 * p], 0.0).astype(BF16)
        w_hi = jnp.where(col <= row, ws_ref[2 * p + 1], 0.0).astype(BF16)
        mp = _dot(jnp.concatenate([w_lo, w_hi], axis=1), rhs)
        mixed_cols.append(jnp.concatenate(
            [mp[:, c * LANES:(c + 1) * LANES] for c in range(n_chunks)], axis=0))
    mixed = jnp.concatenate(mixed_cols, axis=1)
    bias = jnp.concatenate([bs_ref[...]] * n_chunks, axis=0)
    gm = (u * (mixed + bias)).astype(BF16)

    y_attn = _dot(attn_ref[...], wba_ref[...])
    y_gmlp = _dot(gm, wbg_ref[...])
    gates = jax.nn.sigmoid(_dot(n, wgate_ref[...]) + bgate_ref[...])
    merged = gates[:, :D_MODEL] * y_attn + gates[:, D_MODEL:] * y_gmlp
    o_ref[...] = h + _dot(merged.astype(BF16), wout_ref[...])


def _resident(shape):
    zeros = (0,) * len(shape)
    return pl.BlockSpec(shape, lambda *_: zeros, pipeline_mode=pl.Buffered(1))


def _row_tiles(width):
    return pl.BlockSpec((TOKEN_TILE, width), lambda i: (i, 0))


def _dense_params():
    return pltpu.CompilerParams(dimension_semantics=("parallel",), vmem_limit_bytes=VMEM_LIMIT_BYTES)


def kernel(x, ffn1_norm, ffn1_w_gate, ffn1_w_up, ffn1_w_down, mix_norm, w_in, gmlp_ln_g, gmlp_ln_b, gmlp_w_s, gmlp_b_s, w_branch_attn, w_branch_gmlp, w_gate, b_gate, w_out, ffn2_norm, ffn2_w_gate, ffn2_w_up, ffn2_w_down, final_norm):
    B, S, D = x.shape
    T = B * S
    n_blocks = S // MOBA_BLOCK
    assert D == D_MODEL and S % MOBA_BLOCK == 0 and T % TOKEN_TILE == 0
    assert TOKEN_TILE % MOBA_BLOCK == 0 and TOKEN_TILE % G_CHUNK == 0
    assert n_blocks <= LANES, "one one-hot lane per key block"
    assert ffn1_norm.shape[0] == 1, "single-layer block"
    vec = lambda w: w.reshape(1, -1).astype(F32)
    n_tiles = T // TOKEN_TILE
    x2 = x.reshape(T, D)

    first = [(ffn1_w_gate, D_FF), (ffn1_w_up, D_FF), (ffn1_w_down, D), (w_in, 3 * A_WIDTH)]
    assert all(w.shape[1] % (FIRST_CAST_BLOCKS * BF16_ROWS) == 0 for w, _ in first)
    ffn1_wg, ffn1_wu, ffn1_wd, w_qkv = pl.pallas_call(
        _cast_kernel,
        grid=(FIRST_CAST_BLOCKS,),
        in_specs=[pl.BlockSpec((None, w.shape[1] // FIRST_CAST_BLOCKS, cols), lambda i: (0, i, 0))
                  for w, cols in first],
        out_specs=[pl.BlockSpec((w.shape[1] // FIRST_CAST_BLOCKS, cols), lambda i: (i, 0)) for w, cols in first],
        out_shape=[jax.ShapeDtypeStruct((w.shape[1], cols), BF16) for w, cols in first],
        compiler_params=_dense_params(),
        name="cast_first",
    )(*[w for w, _ in first])

    later = [ffn2_w_gate, ffn2_w_up, ffn2_w_down, w_gate, w_out, w_branch_attn, w_branch_gmlp]
    steps_per_block, rest = divmod(n_tiles, SIDE_CAST_BLOCKS)
    assert rest == 0 and all(w.shape[1] % (SIDE_CAST_BLOCKS * BF16_ROWS) == 0 for w in later + [w_in])
    row_block = lambda i: i // steps_per_block
    side_in_specs = [pl.BlockSpec((None, w.shape[1] // SIDE_CAST_BLOCKS, w.shape[2]), lambda i: (0, row_block(i), 0))
                     for w in later]
    side_out_specs = [pl.BlockSpec((w.shape[1] // SIDE_CAST_BLOCKS, w.shape[2]), lambda i: (row_block(i), 0))
                      for w in later]
    side_out_shapes = [jax.ShapeDtypeStruct(w.shape[1:], BF16) for w in later]
    assert (3 * A_WIDTH) % G_WIDTH == 0
    first_g = 3 * A_WIDTH // G_WIDTH
    side_in_specs += [pl.BlockSpec((None, D // SIDE_CAST_BLOCKS, G_WIDTH), lambda i, c=c: (0, row_block(i), first_g + c))
                      for c in range(2)]
    side_out_specs.append(pl.BlockSpec((D // SIDE_CAST_BLOCKS, 2 * G_WIDTH), lambda i: (row_block(i), 0)))
    side_out_shapes.append(jax.ShapeDtypeStruct((D, 2 * G_WIDTH), BF16))
    h1, qkv, kmean, *later_bf16 = pl.pallas_call(
        _ffn1_qkv_kernel,
        grid=(n_tiles,),
        in_specs=[_row_tiles(D), _resident((1, D)), _resident((D, D_FF)), _resident((D, D_FF)),
                  _resident((D_FF, D)), _resident((1, D)), _resident((D, 3 * A_WIDTH))] + side_in_specs,
        out_specs=[_row_tiles(D), _row_tiles(3 * A_WIDTH),
                   pl.BlockSpec((TOKEN_TILE // MOBA_BLOCK, 1, A_WIDTH), lambda i: (i, 0, 0))] + side_out_specs,
        out_shape=[jax.ShapeDtypeStruct((T, D), F32), jax.ShapeDtypeStruct((T, 3 * A_WIDTH), BF16),
                   jax.ShapeDtypeStruct((T // MOBA_BLOCK, 1, A_WIDTH), F32)] + side_out_shapes,
        compiler_params=pltpu.CompilerParams(dimension_semantics=("arbitrary",),
                                             vmem_limit_bytes=VMEM_LIMIT_BYTES),
        name="ffn1_qkv",
    )(x2, vec(ffn1_norm), ffn1_wg, ffn1_wu, ffn1_wd, vec(mix_norm), w_qkv, *later, w_in, w_in)
    ffn2_wg, ffn2_wu, ffn2_wd, w_gate_bf, w_out_bf, w_ba_bf, w_bg_bf, w_uv_bf = later_bf16

    qkv3 = qkv.reshape(B, S, 3 * A_WIDTH)
    attn = pl.pallas_call(
        _moba_kernel,
        grid=(B, n_blocks),
        in_specs=[pl.BlockSpec((1, MOBA_BLOCK, A_WIDTH), lambda b, i: (b, i, 0)),
                  pl.BlockSpec((1, S, A_WIDTH), lambda b, i: (b, 0, 1)),
                  pl.BlockSpec((1, S, A_WIDTH), lambda b, i: (b, 0, 2)),
                  pl.BlockSpec((1, n_blocks, A_WIDTH), lambda b, i: (b, 0, 0))],
        out_specs=pl.BlockSpec((1, MOBA_BLOCK, A_WIDTH), lambda b, i: (b, i, 0)),
        out_shape=jax.ShapeDtypeStruct((B, S, A_WIDTH), BF16),
        scratch_shapes=[pltpu.VMEM((A_HEADS * V_ROWS, S), BF16),
                        pltpu.VMEM((A_HEADS, 2 * LANES, MOBA_BLOCK), BF16),
                        pltpu.VMEM((2, A_HEADS, MOBA_BLOCK, MOBA_BLOCK), F32),
                        pltpu.VMEM((2, A_HEADS, 1, MOBA_BLOCK), F32),
                        pltpu.VMEM((A_HEADS, V_ROWS, MOBA_BLOCK), F32)],
        compiler_params=pltpu.CompilerParams(dimension_semantics=("parallel", "arbitrary"),
                                             vmem_limit_bytes=VMEM_LIMIT_BYTES),
        name="moba_attn",
    )(qkv3, qkv3, qkv3, kmean.reshape(B, n_blocks, A_WIDTH))

    bs_lanes = jnp.repeat(gmlp_b_s[0].T, G_GROUP_DIM, axis=1).astype(F32)
    h2 = pl.pallas_call(
        _mix_kernel,
        grid=(n_tiles,),
        in_specs=[_row_tiles(D), _row_tiles(A_WIDTH), _resident((1, D)), _resident((D, 2 * G_WIDTH)),
                  _resident((1, G_WIDTH)), _resident((1, G_WIDTH)),
                  _resident((G_GROUPS, G_CHUNK, G_CHUNK)), _resident((G_CHUNK, G_WIDTH)),
                  _resident((A_WIDTH, D)), _resident((G_WIDTH, D)), _resident((D, 2 * D)),
                  _resident((1, 2 * D)), _resident((D, D))],
        out_specs=_row_tiles(D),
        out_shape=jax.ShapeDtypeStruct((T, D), F32),
        compiler_params=_dense_params(),
        name="mix",
    )(h1, attn.reshape(T, A_WIDTH), vec(mix_norm), w_uv_bf, vec(gmlp_ln_g),
      vec(gmlp_ln_b), gmlp_w_s[0].astype(F32), bs_lanes, w_ba_bf, w_bg_bf,
      w_gate_bf, vec(b_gate), w_out_bf)

    out = pl.pallas_call(
        _ffn2_out_kernel,
        grid=(n_tiles,),
        in_specs=[_row_tiles(D), _resident((1, D)), _resident((D, D_FF)), _resident((D, D_FF)),
                  _resident((D_FF, D)), _resident((1, D))],
        out_specs=_row_tiles(D),
        out_shape=jax.ShapeDtypeStruct((T, D), F32),
        compiler_params=_dense_params(),
        name="ffn2_out",
    )(h2, vec(ffn2_norm), ffn2_wg, ffn2_wu, ffn2_wd, vec(final_norm))
    return out.reshape(B, S, D)
```

```python
import jax
import jax.numpy as jnp
from jax import lax
from jax.experimental import pallas as pl
from jax.experimental.pallas import tpu as pltpu

D_MODEL = 1024
D_FF = 2816
A_HEADS = 8
A_HEAD_DIM = 64
A_WIDTH = A_HEADS * A_HEAD_DIM
MOBA_BLOCK = 256
MOBA_TOPK = 3
G_GROUPS = 8
G_CHUNK = 128
G_WIDTH = 512
G_GROUP_DIM = G_WIDTH // G_GROUPS
EPS = 1e-6

LANES = 128
BF16_ROWS = 16
V_ROWS = A_HEAD_DIM + BF16_ROWS
HEAD_PAIRS = A_WIDTH // LANES
GROUP_PAIRS = G_WIDTH // LANES
MXU_COLS = 256
FF_CHUNKS = (768, 768, 768, 512)
assert sum(FF_CHUNKS) == D_FF and all(c % MXU_COLS == 0 for c in FF_CHUNKS)

TOKEN_TILE = 512
MIX_ROW_GROUPS = 2
SIDE_CAST_BLOCKS = 16
FIRST_CAST_BLOCKS = 8
VMEM_LIMIT_BYTES = 56 * 1024 * 1024
MASKED = -1e30
LOG2_E = 1.4426950408889634
LONG_RUN = 4
F32 = jnp.float32
BF16 = jnp.bfloat16


def _dot(a, b):
    return jnp.dot(a, b, preferred_element_type=F32)


def _rmsnorm(x, g):
    return x * lax.rsqrt(jnp.mean(x * x, axis=-1, keepdims=True) + EPS) * g


def _swiglu_residual(x, norm_g, wg_ref, wu_ref, wd_ref):
    n = _rmsnorm(x, norm_g).astype(BF16)
    acc = jnp.zeros(x.shape, F32)
    start = 0
    for width in FF_CHUNKS:
        gate = _dot(n, wg_ref[:, start:start + width])
        up = _dot(n, wu_ref[:, start:start + width])
        act = (gate * jax.nn.sigmoid(gate) * up).astype(BF16)
        acc = acc + _dot(act, wd_ref[start:start + width, :])
        start += width
    return x + 0.5 * acc


def _ffn1_qkv_kernel(x_ref, g1_ref, wg_ref, wu_ref, wd_ref, gmix_ref, wqkv_ref, *rest):
    k = (len(rest) - 4) // 2
    side_in, (h_ref, qkv_ref, kmean_ref), side_out = rest[:k + 1], rest[k + 1:k + 4], rest[k + 4:]
    for src, dst in zip(side_in[:-2], side_out[:-1]):
        dst[...] = src[...].astype(BF16)
    half = side_in[-1].shape[-1]
    side_out[-1][:, :half] = side_in[-2][...].astype(BF16)
    side_out[-1][:, half:] = side_in[-1][...].astype(BF16)
    h = _swiglu_residual(x_ref[...], g1_ref[...], wg_ref, wu_ref, wd_ref)
    h_ref[...] = h
    n = _rmsnorm(h, gmix_ref[...]).astype(BF16)
    z = _dot(n, wqkv_ref[...])
    qkv_ref[...] = z.astype(BF16)
    for j in range(TOKEN_TILE // MOBA_BLOCK):
        kb = z[j * MOBA_BLOCK:(j + 1) * MOBA_BLOCK, A_WIDTH:2 * A_WIDTH]
        kmean_ref[j] = jnp.mean(kb, axis=0, keepdims=True)


def _cast_kernel(*refs):
    n = len(refs) // 2
    for src, dst in zip(refs[:n], refs[n:]):
        dst[...] = src[...].astype(BF16)


def _ffn2_out_kernel(h_ref, g2_ref, wg_ref, wu_ref, wd_ref, gfin_ref, o_ref):
    h = _swiglu_residual(h_ref[...], g2_ref[...], wg_ref, wu_ref, wd_ref)
    o_ref[...] = _rmsnorm(h, gfin_ref[...])


def _moba_kernel(q_ref, k_ref, v_ref, kmean_ref, o_ref, vta_ref, qta_ref, s_ref, smax_ref, acc_ref):
    qi = pl.program_id(1)
    seq = k_ref.shape[1]
    n_blocks = seq // MOBA_BLOCK
    scale = A_HEAD_DIM ** -0.5

    @pl.when(qi == 0)
    def _():
        ones_row = (lax.broadcasted_iota(jnp.int32, (BF16_ROWS, seq), 0) == 0).astype(BF16)
        for h in range(A_HEADS):
            vta_ref[h * V_ROWS + A_HEAD_DIM:(h + 1) * V_ROWS, :] = ones_row
            qta_ref[h, LANES + n_blocks:, :] = jnp.zeros((LANES - n_blocks, MOBA_BLOCK), BF16)
        for n in range(n_blocks):
            blk = slice(n * MOBA_BLOCK, (n + 1) * MOBA_BLOCK)
            vt = v_ref[0, blk, :].astype(F32).T.astype(BF16)
            for h in range(A_HEADS):
                vta_ref[h * V_ROWS:h * V_ROWS + A_HEAD_DIM, blk] = vt[h * A_HEAD_DIM:(h + 1) * A_HEAD_DIM, :]

    lane = lax.broadcasted_iota(jnp.int32, (MOBA_BLOCK, LANES), 1)
    feat = lax.broadcasted_iota(jnp.int32, (LANES, MOBA_BLOCK), 0)
    blk_row = lax.broadcasted_iota(jnp.int32, (n_blocks, MOBA_BLOCK), 0)
    key_pos = lax.broadcasted_iota(jnp.int32, (MOBA_BLOCK, MOBA_BLOCK), 0)
    qry_pos = lax.broadcasted_iota(jnp.int32, (MOBA_BLOCK, MOBA_BLOCK), 1)
    valid = blk_row < qi

    def score_head(n, slot, h, own=False):
        start = pl.multiple_of(n * MOBA_BLOCK, MOBA_BLOCK)
        k_pair = k_ref[0, pl.ds(start, MOBA_BLOCK), (h // 2) * LANES:(h // 2 + 1) * LANES]
        if own:
            s = jnp.where(key_pos <= qry_pos, _dot(k_pair, qta_ref[h, :LANES, :]), MASKED)
        else:
            s = _dot(jnp.concatenate([k_pair, (lane == n).astype(BF16)], axis=1), qta_ref[h])
        s_ref[slot, h] = s
        smax_ref[slot, h] = jnp.max(s, axis=0, keepdims=True)

    for pair in range(HEAD_PAIRS):
        cols = slice(pair * LANES, (pair + 1) * LANES)
        qt = (q_ref[0, :, cols].astype(F32) * scale).T
        kmean = kmean_ref[0, :, cols].astype(BF16)
        for half in range(2):
            h = 2 * pair + half
            in_head = (feat >= half * A_HEAD_DIM) & (feat < (half + 1) * A_HEAD_DIM)
            qth = jnp.where(in_head, qt, 0.0).astype(BF16)
            qta_ref[h, :LANES, :] = jnp.where(in_head, qt * LOG2_E, 0.0).astype(BF16)
            score_head(qi, 0, h, own=True)
            gate = jnp.where(valid, _dot(kmean, qth), -jnp.inf)
            selected = jnp.zeros(gate.shape, jnp.bool_)
            for _ in range(MOBA_TOPK):
                best = jnp.max(gate, axis=0, keepdims=True)
                first = jnp.min(jnp.where(gate == best, blk_row, n_blocks), axis=0, keepdims=True)
                pick = blk_row == first
                selected = selected | pick
                gate = jnp.where(pick, -jnp.inf, gate)
            selected = selected & valid
            qta_ref[h, LANES:LANES + n_blocks, :] = jnp.where(selected, 0.0, MASKED).astype(BF16)

    def update_head(n, slot, h, m_prev):
        start = pl.multiple_of(n * MOBA_BLOCK, MOBA_BLOCK)
        m_new = smax_ref[slot, h]
        if m_prev is not None:
            m_new = jnp.maximum(m_prev, m_new)
        p = jnp.exp2(s_ref[slot, h] - m_new).astype(BF16)
        pv = _dot(vta_ref[h * V_ROWS:(h + 1) * V_ROWS, pl.ds(start, MOBA_BLOCK)], p)
        if m_prev is None:
            acc_ref[h] = pv
        else:
            acc_ref[h] = jnp.exp2(m_prev - m_new) * acc_ref[h] + pv
        return m_new

    def advance(blocks, m_run):
        for n, slot, n_next in blocks:
            m_out = []
            for h in range(A_HEADS):
                if n_next is not None:
                    score_head(n_next, 1 - slot, h)
                m_out.append(update_head(n, slot, h, None if m_run is None else m_run[h]))
            m_run = m_out
        return tuple(m_run)

    m_run = advance([(qi, 0, 0)], None)

    def run_of_blocks(n0, count, m_run):
        return advance([(n0 + i, (i + 1) % 2, n0 + i + 1) for i in range(count)], m_run)

    ahead = jnp.maximum(qi - 1, 0)
    long_trips = lax.shift_right_logical(ahead, LONG_RUN.bit_length() - 1)
    pairs_start = long_trips * LONG_RUN
    pair_trips = lax.shift_right_logical(ahead - pairs_start, 1)
    single_start = pairs_start + 2 * pair_trips
    m_run = lax.fori_loop(0, long_trips, lambda j, m: run_of_blocks(j * LONG_RUN, LONG_RUN, m), m_run)
    m_run = lax.fori_loop(0, pair_trips, lambda j, m: run_of_blocks(pairs_start + 2 * j, 2, m), m_run)
    m_run = lax.fori_loop(0, ahead - single_start, lambda j, m: run_of_blocks(single_start, 1, m), m_run)
    last_is_even = (ahead & 1) == 0
    m_run = lax.fori_loop(0, jnp.where((qi > 0) & last_is_even, 1, 0),
                          lambda j, m: advance([(ahead, 1, None)], m), m_run)
    lax.fori_loop(0, jnp.where((qi > 0) & ~last_is_even, 1, 0),
                  lambda j, m: advance([(ahead, 0, None)], m), m_run)

    for pair in range(HEAD_PAIRS):
        outs = []
        for h in (2 * pair, 2 * pair + 1):
            acc = acc_ref[h]
            outs.append(acc[:A_HEAD_DIM] / acc[A_HEAD_DIM:A_HEAD_DIM + 1])
        o_ref[0, :, pair * LANES:(pair + 1) * LANES] = jnp.concatenate(outs, axis=0).T.astype(o_ref.dtype)


def _mix_kernel(h_ref, attn_ref, gmix_ref, wuv_ref, lng_ref, lnb_ref, ws_ref, bs_ref,
                wba_ref, wbg_ref, wgate_ref, bgate_ref, wout_ref, o_ref):
    rows = h_ref.shape[0] // MIX_ROW_GROUPS
    n_chunks = rows // G_CHUNK
    row = lax.broadcasted_iota(jnp.int32, (G_CHUNK, G_CHUNK), 0)
    col = lax.broadcasted_iota(jnp.int32, (G_CHUNK, G_CHUNK), 1)
    lane = lax.broadcasted_iota(jnp.int32, (G_CHUNK, n_chunks * LANES), 1)
    low_half = (lane % LANES) < G_GROUP_DIM

    staged = []
    for r in range(MIX_ROW_GROUPS):
        sl = slice(r * rows, (r + 1) * rows)
        h = h_ref[sl, :]
        n = _rmsnorm(h, gmix_ref[...]).astype(BF16)
        staged.append((sl, h, _dot(n, wuv_ref[...]), _dot(n, wgate_ref[...])))

    for sl, h, z_uv, gate_logits in staged:
        zg = jax.nn.gelu(z_uv, approximate=True)
        u = zg[:, :G_WIDTH]
        v = zg[:, G_WIDTH:]
        mu = jnp.mean(v, axis=-1, keepdims=True)
        var = jnp.mean(jnp.square(v - mu), axis=-1, keepdims=True)
        v = ((v - mu) * lax.rsqrt(var + EPS) * lng_ref[...] + lnb_ref[...]).astype(BF16)
        mixed_cols = []
        for p in range(GROUP_PAIRS):
            vp = jnp.concatenate(
                [v[c * G_CHUNK:(c + 1) * G_CHUNK, p * LANES:(p + 1) * LANES] for c in range(n_chunks)], axis=1)
            zeros = jnp.zeros_like(vp)
            rhs = jnp.concatenate([jnp.where(low_half, vp, zeros), jnp.where(low_half, zeros, vp)], axis=0)
            w_lo = jnp.where(col <= row, ws_ref[2 * p], 0.0).astype(BF16)
            w_hi = jnp.where(col <= row, ws_ref[2 * p + 1], 0.0).astype(BF16)
            mp = _dot(jnp.concatenate([w_lo, w_hi], axis=1), rhs)
            mixed_cols.append(jnp.concatenate(
                [mp[:, c * LANES:(c + 1) * LANES] for c in range(n_chunks)], axis=0))
        mixed = jnp.concatenate(mixed_cols, axis=1)
        bias = jnp.concatenate([bs_ref[...]] * n_chunks, axis=0)
        gm = (u * (mixed + bias)).astype(BF16)

        y_attn = _dot(attn_ref[sl, :], wba_ref[...])
        y_gmlp = _dot(gm, wbg_ref[...])
        gates = jax.nn.sigmoid(gate_logits + bgate_ref[...])
        merged = gates[:, :D_MODEL] * y_attn + gates[:, D_MODEL:] * y_gmlp
        o_ref[sl, :] = h + _dot(merged.astype(BF16), wout_ref[...])


def _resident(shape):
    zeros = (0,) * len(shape)
    return pl.BlockSpec(shape, lambda *_: zeros, pipeline_mode=pl.Buffered(1))


def _row_tiles(width):
    return pl.BlockSpec((TOKEN_TILE, width), lambda i: (i, 0))


def _dense_params():
    return pltpu.CompilerParams(dimension_semantics=("parallel",), vmem_limit_bytes=VMEM_LIMIT_BYTES)


def kernel(x, ffn1_norm, ffn1_w_gate, ffn1_w_up, ffn1_w_down, mix_norm, w_in, gmlp_ln_g, gmlp_ln_b, gmlp_w_s, gmlp_b_s, w_branch_attn, w_branch_gmlp, w_gate, b_gate, w_out, ffn2_norm, ffn2_w_gate, ffn2_w_up, ffn2_w_down, final_norm):
    B, S, D = x.shape
    T = B * S
    n_blocks = S // MOBA_BLOCK
    assert D == D_MODEL and S % MOBA_BLOCK == 0 and T % TOKEN_TILE == 0
    assert TOKEN_TILE % MOBA_BLOCK == 0 and TOKEN_TILE % G_CHUNK == 0
    assert n_blocks <= LANES, "one one-hot lane per key block"
    assert ffn1_norm.shape[0] == 1, "single-layer block"
    vec = lambda w: w.reshape(1, -1).astype(F32)
    n_tiles = T // TOKEN_TILE
    x2 = x.reshape(T, D)

    first = [(ffn1_w_gate, D_FF), (ffn1_w_up, D_FF), (ffn1_w_down, D), (w_in, 3 * A_WIDTH)]
    assert all(w.shape[1] % (FIRST_CAST_BLOCKS * BF16_ROWS) == 0 for w, _ in first)
    ffn1_wg, ffn1_wu, ffn1_wd, w_qkv = pl.pallas_call(
        _cast_kernel,
        grid=(FIRST_CAST_BLOCKS,),
        in_specs=[pl.BlockSpec((None, w.shape[1] // FIRST_CAST_BLOCKS, cols), lambda i: (0, i, 0))
                  for w, cols in first],
        out_specs=[pl.BlockSpec((w.shape[1] // FIRST_CAST_BLOCKS, cols), lambda i: (i, 0)) for w, cols in first],
        out_shape=[jax.ShapeDtypeStruct((w.shape[1], cols), BF16) for w, cols in first],
        compiler_params=_dense_params(),
        name="cast_first",
    )(*[w for w, _ in first])

    later = [ffn2_w_gate, ffn2_w_up, ffn2_w_down, w_gate, w_out, w_branch_attn, w_branch_gmlp]
    steps_per_block, rest = divmod(n_tiles, SIDE_CAST_BLOCKS)
    assert rest == 0 and all(w.shape[1] % (SIDE_CAST_BLOCKS * BF16_ROWS) == 0 for w in later + [w_in])
    row_block = lambda i: i // steps_per_block
    side_in_specs = [pl.BlockSpec((None, w.shape[1] // SIDE_CAST_BLOCKS, w.shape[2]), lambda i: (0, row_block(i), 0))
                     for w in later]
    side_out_specs = [pl.BlockSpec((w.shape[1] // SIDE_CAST_BLOCKS, w.shape[2]), lambda i: (row_block(i), 0))
                      for w in later]
    side_out_shapes = [jax.ShapeDtypeStruct(w.shape[1:], BF16) for w in later]
    assert (3 * A_WIDTH) % G_WIDTH == 0
    first_g = 3 * A_WIDTH // G_WIDTH
    side_in_specs += [pl.BlockSpec((None, D // SIDE_CAST_BLOCKS, G_WIDTH), lambda i, c=c: (0, row_block(i), first_g + c))
                      for c in range(2)]
    side_out_specs.append(pl.BlockSpec((D // SIDE_CAST_BLOCKS, 2 * G_WIDTH), lambda i: (row_block(i), 0)))
    side_out_shapes.append(jax.ShapeDtypeStruct((D, 2 * G_WIDTH), BF16))
    h1, qkv, kmean, *later_bf16 = pl.pallas_call(
        _ffn1_qkv_kernel,
        grid=(n_tiles,),
        in_specs=[_row_tiles(D), _resident((1, D)), _resident((D, D_FF)), _resident((D, D_FF)),
                  _resident((D_FF, D)), _resident((1, D)), _resident((D, 3 * A_WIDTH))] + side_in_specs,
        out_specs=[_row_tiles(D), _row_tiles(3 * A_WIDTH),
                   pl.BlockSpec((TOKEN_TILE // MOBA_BLOCK, 1, A_WIDTH), lambda i: (i, 0, 0))] + side_out_specs,
        out_shape=[jax.ShapeDtypeStruct((T, D), F32), jax.ShapeDtypeStruct((T, 3 * A_WIDTH), BF16),
                   jax.ShapeDtypeStruct((T // MOBA_BLOCK, 1, A_WIDTH), F32)] + side_out_shapes,
        compiler_params=pltpu.CompilerParams(dimension_semantics=("arbitrary",),
                                             vmem_limit_bytes=VMEM_LIMIT_BYTES),
        name="ffn1_qkv",
    )(x2, vec(ffn1_norm), ffn1_wg, ffn1_wu, ffn1_wd, vec(mix_norm), w_qkv, *later, w_in, w_in)
    ffn2_wg, ffn2_wu, ffn2_wd, w_gate_bf, w_out_bf, w_ba_bf, w_bg_bf, w_uv_bf = later_bf16

    qkv3 = qkv.reshape(B, S, 3 * A_WIDTH)
    attn = pl.pallas_call(
        _moba_kernel,
        grid=(B, n_blocks),
        in_specs=[pl.BlockSpec((1, MOBA_BLOCK, A_WIDTH), lambda b, i: (b, i, 0)),
                  pl.BlockSpec((1, S, A_WIDTH), lambda b, i: (b, 0, 1)),
                  pl.BlockSpec((1, S, A_WIDTH), lambda b, i: (b, 0, 2)),
                  pl.BlockSpec((1, n_blocks, A_WIDTH), lambda b, i: (b, 0, 0))],
        out_specs=pl.BlockSpec((1, MOBA_BLOCK, A_WIDTH), lambda b, i: (b, i, 0)),
        out_shape=jax.ShapeDtypeStruct((B, S, A_WIDTH), BF16),
        scratch_shapes=[pltpu.VMEM((A_HEADS * V_ROWS, S), BF16),
                        pltpu.VMEM((A_HEADS, 2 * LANES, MOBA_BLOCK), BF16),
                        pltpu.VMEM((2, A_HEADS, MOBA_BLOCK, MOBA_BLOCK), F32),
                        pltpu.VMEM((2, A_HEADS, 1, MOBA_BLOCK), F32),
                        pltpu.VMEM((A_HEADS, V_ROWS, MOBA_BLOCK), F32)],
        compiler_params=pltpu.CompilerParams(dimension_semantics=("parallel", "arbitrary"),
                                             vmem_limit_bytes=VMEM_LIMIT_BYTES),
        name="moba_attn",
    )(qkv3, qkv3, qkv3, kmean.reshape(B, n_blocks, A_WIDTH))

    bs_lanes = jnp.repeat(gmlp_b_s[0].T, G_GROUP_DIM, axis=1).astype(F32)
    h2 = pl.pallas_call(
        _mix_kernel,
        grid=(n_tiles,),
        in_specs=[_row_tiles(D), _row_tiles(A_WIDTH), _resident((1, D)), _resident((D, 2 * G_WIDTH)),
                  _resident((1, G_WIDTH)), _resident((1, G_WIDTH)),
                  _resident((G_GROUPS, G_CHUNK, G_CHUNK)), _resident((G_CHUNK, G_WIDTH)),
                  _resident((A_WIDTH, D)), _resident((G_WIDTH, D)), _resident((D, 2 * D)),
                  _resident((1, 2 * D)), _resident((D, D))],
        out_specs=_row_tiles(D),
        out_shape=jax.ShapeDtypeStruct((T, D), F32),
        compiler_params=_dense_params(),
        name="mix",
    )(h1, attn.reshape(T, A_WIDTH), vec(mix_norm), w_uv_bf, vec(gmlp_ln_g),
      vec(gmlp_ln_b), gmlp_w_s[0].astype(F32), bs_lanes, w_ba_bf, w_bg_bf,
      w_gate_bf, vec(b_gate), w_out_bf)

    out = pl.pallas_call(
        _ffn2_out_kernel,
        grid=(n_tiles,),
        in_specs=[_row_tiles(D), _resident((1, D)), _resident((D, D_FF)), _resident((D, D_FF)),
                  _resident((D_FF, D)), _resident((1, D))],
        out_specs=_row_tiles(D),
        out_shape=jax.ShapeDtypeStruct((T, D), F32),
        compiler_params=_dense_params(),
        name="ffn2_out",
    )(h2, vec(ffn2_norm), ffn2_wg, ffn2_wu, ffn2_wd, vec(final_norm))
    return out.reshape(B, S, D)
```

```python
import jax
import jax.numpy as jnp
from jax import lax
from jax.experimental import pallas as pl
from jax.experimental.pallas import tpu as pltpu

D_MODEL = 1024
D_FF = 2816
A_HEADS = 8
A_HEAD_DIM = 64
A_WIDTH = A_HEADS * A_HEAD_DIM
MOBA_BLOCK = 256
MOBA_TOPK = 3
G_GROUPS = 8
G_CHUNK = 128
G_WIDTH = 512
G_GROUP_DIM = G_WIDTH // G_GROUPS
EPS = 1e-6

LANES = 128
BF16_ROWS = 16
V_ROWS = A_HEAD_DIM + BF16_ROWS
HEAD_PAIRS = A_WIDTH // LANES
GROUP_PAIRS = G_WIDTH // LANES
MXU_COLS = 256
FF_CHUNKS = (768, 768, 768, 512)
assert sum(FF_CHUNKS) == D_FF and all(c % MXU_COLS == 0 for c in FF_CHUNKS)

TOKEN_TILE = 512
MIX_ROW_GROUPS = 2
SIDE_CAST_BLOCKS = 16
FIRST_CAST_BLOCKS = 8
VMEM_LIMIT_BYTES = 56 * 1024 * 1024
MASKED = -1e30
LOG2_E = 1.4426950408889634
TILE_PAIR = 2
LONG_RUN = 4
F32 = jnp.float32
BF16 = jnp.bfloat16


def _dot(a, b):
    return jnp.dot(a, b, preferred_element_type=F32)


def _rmsnorm(x, g):
    return x * lax.rsqrt(jnp.mean(x * x, axis=-1, keepdims=True) + EPS) * g


def _swiglu_residual(x, norm_g, wg_ref, wu_ref, wd_ref):
    n = _rmsnorm(x, norm_g).astype(BF16)
    acc = jnp.zeros(x.shape, F32)
    start = 0
    for width in FF_CHUNKS:
        gate = _dot(n, wg_ref[:, start:start + width])
        up = _dot(n, wu_ref[:, start:start + width])
        act = (gate * jax.nn.sigmoid(gate) * up).astype(BF16)
        acc = acc + _dot(act, wd_ref[start:start + width, :])
        start += width
    return x + 0.5 * acc


def _ffn1_qkv_kernel(x_ref, g1_ref, wg_ref, wu_ref, wd_ref, gmix_ref, wqkv_ref, *rest):
    k = (len(rest) - 4) // 2
    side_in, (h_ref, qkv_ref, kmean_ref), side_out = rest[:k + 1], rest[k + 1:k + 4], rest[k + 4:]
    for src, dst in zip(side_in[:-2], side_out[:-1]):
        dst[...] = src[...].astype(BF16)
    half = side_in[-1].shape[-1]
    side_out[-1][:, :half] = side_in[-2][...].astype(BF16)
    side_out[-1][:, half:] = side_in[-1][...].astype(BF16)
    h = _swiglu_residual(x_ref[...], g1_ref[...], wg_ref, wu_ref, wd_ref)
    h_ref[...] = h
    n = _rmsnorm(h, gmix_ref[...]).astype(BF16)
    z = _dot(n, wqkv_ref[...])
    qkv_ref[...] = z.astype(BF16)
    for j in range(TOKEN_TILE // MOBA_BLOCK):
        kb = z[j * MOBA_BLOCK:(j + 1) * MOBA_BLOCK, A_WIDTH:2 * A_WIDTH]
        kmean_ref[j] = jnp.mean(kb, axis=0, keepdims=True)


def _cast_kernel(*refs):
    n = len(refs) // 2
    for src, dst in zip(refs[:n], refs[n:]):
        dst[...] = src[...].astype(BF16)


def _ffn2_out_kernel(h_ref, g2_ref, wg_ref, wu_ref, wd_ref, gfin_ref, o_ref):
    h = _swiglu_residual(h_ref[...], g2_ref[...], wg_ref, wu_ref, wd_ref)
    o_ref[...] = _rmsnorm(h, gfin_ref[...])


def _moba_kernel(q_ref, k_ref, v_ref, kmean_ref, o_ref, vta_ref, qta_ref, s_ref, smax_ref, acc_ref):
    j = pl.program_id(1)
    seq = k_ref.shape[1]
    n_blocks = seq // MOBA_BLOCK
    scale = A_HEAD_DIM ** -0.5
    first_q = TILE_PAIR * j
    n_chains = TILE_PAIR * A_HEADS
    all_chains = list(range(n_chains))
    last_tile_chains = all_chains[-A_HEADS:]

    @pl.when(j == 0)
    def _():
        ones_row = (lax.broadcasted_iota(jnp.int32, (BF16_ROWS, seq), 0) == 0).astype(BF16)
        for h in range(A_HEADS):
            vta_ref[h * V_ROWS + A_HEAD_DIM:(h + 1) * V_ROWS, :] = ones_row
        for c in all_chains:
            qta_ref[c, LANES + n_blocks:, :] = jnp.zeros((LANES - n_blocks, MOBA_BLOCK), BF16)
        for n in range(n_blocks):
            blk = slice(n * MOBA_BLOCK, (n + 1) * MOBA_BLOCK)
            vt = v_ref[0, blk, :].astype(F32).T.astype(BF16)
            for h in range(A_HEADS):
                vta_ref[h * V_ROWS:h * V_ROWS + A_HEAD_DIM, blk] = vt[h * A_HEAD_DIM:(h + 1) * A_HEAD_DIM, :]

    lane = lax.broadcasted_iota(jnp.int32, (MOBA_BLOCK, LANES), 1)
    feat = lax.broadcasted_iota(jnp.int32, (LANES, MOBA_BLOCK), 0)
    blk_row = lax.broadcasted_iota(jnp.int32, (n_blocks, MOBA_BLOCK), 0)
    key_pos = lax.broadcasted_iota(jnp.int32, (MOBA_BLOCK, MOBA_BLOCK), 0)
    qry_pos = lax.broadcasted_iota(jnp.int32, (MOBA_BLOCK, MOBA_BLOCK), 1)

    def score_head(n, slot, c, own=False):
        pair = (c % A_HEADS) // 2
        start = pl.multiple_of(n * MOBA_BLOCK, MOBA_BLOCK)
        k_pair = k_ref[0, pl.ds(start, MOBA_BLOCK), pair * LANES:(pair + 1) * LANES]
        if own:
            s = jnp.where(key_pos <= qry_pos, _dot(k_pair, qta_ref[c, :LANES, :]), MASKED)
        else:
            s = _dot(jnp.concatenate([k_pair, (lane == n).astype(BF16)], axis=1), qta_ref[c])
        s_ref[slot, c] = s
        smax_ref[slot, c] = jnp.max(s, axis=0, keepdims=True)

    for tile in range(TILE_PAIR):
        qi = first_q + tile
        valid = blk_row < qi
        rows = slice(tile * MOBA_BLOCK, (tile + 1) * MOBA_BLOCK)
        for pair in range(HEAD_PAIRS):
            cols = slice(pair * LANES, (pair + 1) * LANES)
            qt = (q_ref[0, rows, cols].astype(F32) * scale).T
            kmean = kmean_ref[0, :, cols].astype(BF16)
            for half in range(2):
                c = tile * A_HEADS + 2 * pair + half
                in_head = (feat >= half * A_HEAD_DIM) & (feat < (half + 1) * A_HEAD_DIM)
                qth = jnp.where(in_head, qt, 0.0).astype(BF16)
                qta_ref[c, :LANES, :] = jnp.where(in_head, qt * LOG2_E, 0.0).astype(BF16)
                score_head(qi, 0, c, own=True)
                gate = jnp.where(valid, _dot(kmean, qth), -jnp.inf)
                selected = jnp.zeros(gate.shape, jnp.bool_)
                for _ in range(MOBA_TOPK):
                    best = jnp.max(gate, axis=0, keepdims=True)
                    first = jnp.min(jnp.where(gate == best, blk_row, n_blocks), axis=0, keepdims=True)
                    pick = blk_row == first
                    selected = selected | pick
                    gate = jnp.where(pick, -jnp.inf, gate)
                selected = selected & valid
                qta_ref[c, LANES:LANES + n_blocks, :] = jnp.where(selected, 0.0, MASKED).astype(BF16)

    def update_head(n, slot, c, m_prev):
        h = c % A_HEADS
        start = pl.multiple_of(n * MOBA_BLOCK, MOBA_BLOCK)
        m_new = smax_ref[slot, c]
        if m_prev is not None:
            m_new = jnp.maximum(m_prev, m_new)
        p = jnp.exp2(s_ref[slot, c] - m_new).astype(BF16)
        pv = _dot(vta_ref[h * V_ROWS:(h + 1) * V_ROWS, pl.ds(start, MOBA_BLOCK)], p)
        if m_prev is None:
            acc_ref[c] = pv
        else:
            acc_ref[c] = jnp.exp2(m_prev - m_new) * acc_ref[c] + pv
        return m_new

    def advance(n, slot, chains, m_run, n_next=None, chains_next=()):
        m_out = list(m_run) if m_run is not None else [None] * n_chains
        todo = list(chains_next)
        for c in chains:
            if todo:
                score_head(n_next, 1 - slot, todo.pop(0))
            m_out[c] = update_head(n(c) if callable(n) else n, slot, c, None if m_run is None else m_run[c])
        for c in todo:
            score_head(n_next, 1 - slot, c)
        return tuple(m_out)

    m_run = advance(lambda c: first_q + c // A_HEADS, 0, all_chains, None, 0, all_chains)

    def run_of_blocks(n0, count, m_run):
        for i in range(count):
            m_run = advance(n0 + i, (i + 1) % 2, all_chains, m_run, n0 + i + 1, all_chains)
        return m_run

    ahead = jnp.maximum(first_q - 1, 0)
    long_trips = lax.shift_right_logical(ahead, LONG_RUN.bit_length() - 1)
    pairs_start = long_trips * LONG_RUN
    pair_trips = lax.shift_right_logical(ahead - pairs_start, 1)
    single_start = pairs_start + 2 * pair_trips
    m_run = lax.fori_loop(0, long_trips, lambda i, m: run_of_blocks(i * LONG_RUN, LONG_RUN, m), m_run)
    m_run = lax.fori_loop(0, pair_trips, lambda i, m: run_of_blocks(pairs_start + 2 * i, 2, m), m_run)
    m_run = lax.fori_loop(0, ahead - single_start, lambda i, m: run_of_blocks(single_start, 1, m), m_run)
    m_run = lax.fori_loop(0, jnp.where(j > 0, 1, 0),
                          lambda i, m: advance(ahead, 0, all_chains, m, first_q, last_tile_chains), m_run)
    advance(first_q, 1, last_tile_chains, m_run)

    for tile in range(TILE_PAIR):
        rows = slice(tile * MOBA_BLOCK, (tile + 1) * MOBA_BLOCK)
        for pair in range(HEAD_PAIRS):
            outs = []
            for half in range(2):
                acc = acc_ref[tile * A_HEADS + 2 * pair + half]
                outs.append(acc[:A_HEAD_DIM] / acc[A_HEAD_DIM:A_HEAD_DIM + 1])
            o_ref[0, rows, pair * LANES:(pair + 1) * LANES] = jnp.concatenate(outs, axis=0).T.astype(o_ref.dtype)


def _mix_kernel(h_ref, attn_ref, gmix_ref, wuv_ref, lng_ref, lnb_ref, ws_ref, bs_ref,
                wba_ref, wbg_ref, wgate_ref, bgate_ref, wout_ref, o_ref):
    rows = h_ref.shape[0] // MIX_ROW_GROUPS
    n_chunks = rows // G_CHUNK
    row = lax.broadcasted_iota(jnp.int32, (G_CHUNK, G_CHUNK), 0)
    col = lax.broadcasted_iota(jnp.int32, (G_CHUNK, G_CHUNK), 1)
    lane = lax.broadcasted_iota(jnp.int32, (G_CHUNK, n_chunks * LANES), 1)
    low_half = (lane % LANES) < G_GROUP_DIM

    staged = []
    for r in range(MIX_ROW_GROUPS):
        sl = slice(r * rows, (r + 1) * rows)
        h = h_ref[sl, :]
        n = _rmsnorm(h, gmix_ref[...]).astype(BF16)
        staged.append((sl, h, _dot(n, wuv_ref[...]), _dot(n, wgate_ref[...])))

    for sl, h, z_uv, gate_logits in staged:
        zg = jax.nn.gelu(z_uv, approximate=True)
        u = zg[:, :G_WIDTH]
        v = zg[:, G_WIDTH:]
        mu = jnp.mean(v, axis=-1, keepdims=True)
        var = jnp.mean(jnp.square(v - mu), axis=-1, keepdims=True)
        v = ((v - mu) * lax.rsqrt(var + EPS) * lng_ref[...] + lnb_ref[...]).astype(BF16)
        mixed_cols = []
        for p in range(GROUP_PAIRS):
            vp = jnp.concatenate(
                [v[c * G_CHUNK:(c + 1) * G_CHUNK, p * LANES:(p + 1) * LANES] for c in range(n_chunks)], axis=1)
            zeros = jnp.zeros_like(vp)
            rhs = jnp.concatenate([jnp.where(low_half, vp, zeros), jnp.where(low_half, zeros, vp)], axis=0)
            w_lo = jnp.where(col <= row, ws_ref[2 * p], 0.0).astype(BF16)
            w_hi = jnp.where(col <= row, ws_ref[2 * p + 1], 0.0).astype(BF16)
            mp = _dot(jnp.concatenate([w_lo, w_hi], axis=1), rhs)
            mixed_cols.append(jnp.concatenate(
                [mp[:, c * LANES:(c + 1) * LANES] for c in range(n_chunks)], axis=0))
        mixed = jnp.concatenate(mixed_cols, axis=1)
        bias = jnp.concatenate([bs_ref[...]] * n_chunks, axis=0)
        gm = (u * (mixed + bias)).astype(BF16)

        y_attn = _dot(attn_ref[sl, :], wba_ref[...])
        y_gmlp = _dot(gm, wbg_ref[...])
        gates = jax.nn.sigmoid(gate_logits + bgate_ref[...])
        merged = gates[:, :D_MODEL] * y_attn + gates[:, D_MODEL:] * y_gmlp
        o_ref[sl, :] = h + _dot(merged.astype(BF16), wout_ref[...])


def _resident(shape):
    zeros = (0,) * len(shape)
    return pl.BlockSpec(shape, lambda *_: zeros, pipeline_mode=pl.Buffered(1))


def _row_tiles(width):
    return pl.BlockSpec((TOKEN_TILE, width), lambda i: (i, 0))


def _dense_params():
    return pltpu.CompilerParams(dimension_semantics=("parallel",), vmem_limit_bytes=VMEM_LIMIT_BYTES)


def kernel(x, ffn1_norm, ffn1_w_gate, ffn1_w_up, ffn1_w_down, mix_norm, w_in, gmlp_ln_g, gmlp_ln_b, gmlp_w_s, gmlp_b_s, w_branch_attn, w_branch_gmlp, w_gate, b_gate, w_out, ffn2_norm, ffn2_w_gate, ffn2_w_up, ffn2_w_down, final_norm):
    B, S, D = x.shape
    T = B * S
    n_blocks = S // MOBA_BLOCK
    assert D == D_MODEL and S % MOBA_BLOCK == 0 and T % TOKEN_TILE == 0
    assert TOKEN_TILE % MOBA_BLOCK == 0 and TOKEN_TILE % G_CHUNK == 0
    assert n_blocks <= LANES, "one one-hot lane per key block"
    assert ffn1_norm.shape[0] == 1, "single-layer block"
    vec = lambda w: w.reshape(1, -1).astype(F32)
    n_tiles = T // TOKEN_TILE
    x2 = x.reshape(T, D)

    first = [(ffn1_w_gate, D_FF), (ffn1_w_up, D_FF), (ffn1_w_down, D), (w_in, 3 * A_WIDTH)]
    assert all(w.shape[1] % (FIRST_CAST_BLOCKS * BF16_ROWS) == 0 for w, _ in first)
    ffn1_wg, ffn1_wu, ffn1_wd, w_qkv = pl.pallas_call(
        _cast_kernel,
        grid=(FIRST_CAST_BLOCKS,),
        in_specs=[pl.BlockSpec((None, w.shape[1] // FIRST_CAST_BLOCKS, cols), lambda i: (0, i, 0))
                  for w, cols in first],
        out_specs=[pl.BlockSpec((w.shape[1] // FIRST_CAST_BLOCKS, cols), lambda i: (i, 0)) for w, cols in first],
        out_shape=[jax.ShapeDtypeStruct((w.shape[1], cols), BF16) for w, cols in first],
        compiler_params=_dense_params(),
        name="cast_first",
    )(*[w for w, _ in first])

    later = [ffn2_w_gate, ffn2_w_up, ffn2_w_down, w_gate, w_out, w_branch_attn, w_branch_gmlp]
    steps_per_block, rest = divmod(n_tiles, SIDE_CAST_BLOCKS)
    assert rest == 0 and all(w.shape[1] % (SIDE_CAST_BLOCKS * BF16_ROWS) == 0 for w in later + [w_in])
    row_block = lambda i: i // steps_per_block
    side_in_specs = [pl.BlockSpec((None, w.shape[1] // SIDE_CAST_BLOCKS, w.shape[2]), lambda i: (0, row_block(i), 0))
                     for w in later]
    side_out_specs = [pl.BlockSpec((w.shape[1] // SIDE_CAST_BLOCKS, w.shape[2]), lambda i: (row_block(i), 0))
                      for w in later]
    side_out_shapes = [jax.ShapeDtypeStruct(w.shape[1:], BF16) for w in later]
    assert (3 * A_WIDTH) % G_WIDTH == 0
    first_g = 3 * A_WIDTH // G_WIDTH
    side_in_specs += [pl.BlockSpec((None, D // SIDE_CAST_BLOCKS, G_WIDTH), lambda i, c=c: (0, row_block(i), first_g + c))
                      for c in range(2)]
    side_out_specs.append(pl.BlockSpec((D // SIDE_CAST_BLOCKS, 2 * G_WIDTH), lambda i: (row_block(i), 0)))
    side_out_shapes.append(jax.ShapeDtypeStruct((D, 2 * G_WIDTH), BF16))
    h1, qkv, kmean, *later_bf16 = pl.pallas_call(
        _ffn1_qkv_kernel,
        grid=(n_tiles,),
        in_specs=[_row_tiles(D), _resident((1, D)), _resident((D, D_FF)), _resident((D, D_FF)),
                  _resident((D_FF, D)), _resident((1, D)), _resident((D, 3 * A_WIDTH))] + side_in_specs,
        out_specs=[_row_tiles(D), _row_tiles(3 * A_WIDTH),
                   pl.BlockSpec((TOKEN_TILE // MOBA_BLOCK, 1, A_WIDTH), lambda i: (i, 0, 0))] + side_out_specs,
        out_shape=[jax.ShapeDtypeStruct((T, D), F32), jax.ShapeDtypeStruct((T, 3 * A_WIDTH), BF16),
                   jax.ShapeDtypeStruct((T // MOBA_BLOCK, 1, A_WIDTH), F32)] + side_out_shapes,
        compiler_params=pltpu.CompilerParams(dimension_semantics=("arbitrary",),
                                             vmem_limit_bytes=VMEM_LIMIT_BYTES),
        name="ffn1_qkv",
    )(x2, vec(ffn1_norm), ffn1_wg, ffn1_wu, ffn1_wd, vec(mix_norm), w_qkv, *later, w_in, w_in)
    ffn2_wg, ffn2_wu, ffn2_wd, w_gate_bf, w_out_bf, w_ba_bf, w_bg_bf, w_uv_bf = later_bf16

    qkv3 = qkv.reshape(B, S, 3 * A_WIDTH)
    q_rows = TILE_PAIR * MOBA_BLOCK
    n_chains = TILE_PAIR * A_HEADS
    assert TILE_PAIR == 2 and n_blocks % TILE_PAIR == 0
    attn = pl.pallas_call(
        _moba_kernel,
        grid=(B, n_blocks // TILE_PAIR),
        in_specs=[pl.BlockSpec((1, q_rows, A_WIDTH), lambda b, i: (b, i, 0)),
                  pl.BlockSpec((1, S, A_WIDTH), lambda b, i: (b, 0, 1)),
                  pl.BlockSpec((1, S, A_WIDTH), lambda b, i: (b, 0, 2)),
                  pl.BlockSpec((1, n_blocks, A_WIDTH), lambda b, i: (b, 0, 0))],
        out_specs=pl.BlockSpec((1, q_rows, A_WIDTH), lambda b, i: (b, i, 0)),
        out_shape=jax.ShapeDtypeStruct((B, S, A_WIDTH), BF16),
        scratch_shapes=[pltpu.VMEM((A_HEADS * V_ROWS, S), BF16),
                        pltpu.VMEM((n_chains, 2 * LANES, MOBA_BLOCK), BF16),
                        pltpu.VMEM((2, n_chains, MOBA_BLOCK, MOBA_BLOCK), F32),
                        pltpu.VMEM((2, n_chains, 1, MOBA_BLOCK), F32),
                        pltpu.VMEM((n_chains, V_ROWS, MOBA_BLOCK), F32)],
        compiler_params=pltpu.CompilerParams(dimension_semantics=("parallel", "arbitrary"),
                                             vmem_limit_bytes=VMEM_LIMIT_BYTES),
        name="moba_attn",
    )(qkv3, qkv3, qkv3, kmean.reshape(B, n_blocks, A_WIDTH))

    bs_lanes = jnp.repeat(gmlp_b_s[0].T, G_GROUP_DIM, axis=1).astype(F32)
    h2 = pl.pallas_call(
        _mix_kernel,
        grid=(n_tiles,),
        in_specs=[_row_tiles(D), _row_tiles(A_WIDTH), _resident((1, D)), _resident((D, 2 * G_WIDTH)),
                  _resident((1, G_WIDTH)), _resident((1, G_WIDTH)),
                  _resident((G_GROUPS, G_CHUNK, G_CHUNK)), _resident((G_CHUNK, G_WIDTH)),
                  _resident((A_WIDTH, D)), _resident((G_WIDTH, D)), _resident((D, 2 * D)),
                  _resident((1, 2 * D)), _resident((D, D))],
        out_specs=_row_tiles(D),
        out_shape=jax.ShapeDtypeStruct((T, D), F32),
        compiler_params=_dense_params(),
        name="mix",
    )(h1, attn.reshape(T, A_WIDTH), vec(mix_norm), w_uv_bf, vec(gmlp_ln_g),
      vec(gmlp_ln_b), gmlp_w_s[0].astype(F32), bs_lanes, w_ba_bf, w_bg_bf,
      w_gate_bf, vec(b_gate), w_out_bf)

    out = pl.pallas_call(
        _ffn2_out_kernel,
        grid=(n_tiles,),
        in_specs=[_row_tiles(D), _resident((1, D)), _resident((D, D_FF)), _resident((D, D_FF)),
                  _resident((D_FF, D)), _resident((1, D))],
        out_specs=_row_tiles(D),
        out_shape=jax.ShapeDtypeStruct((T, D), F32),
        compiler_params=_dense_params(),
        name="ffn2_out",
    )(h2, vec(ffn2_norm), ffn2_wg, ffn2_wu, ffn2_wd, vec(final_norm))
    return out.reshape(B, S, D)
```

```python
import jax
import jax.numpy as jnp
from jax import lax
from jax.experimental import pallas as pl
from jax.experimental.pallas import tpu as pltpu

D_MODEL = 1024
D_FF = 2816
A_HEADS = 8
A_HEAD_DIM = 64
A_WIDTH = A_HEADS * A_HEAD_DIM
MOBA_BLOCK = 256
MOBA_TOPK = 3
G_GROUPS = 8
G_CHUNK = 128
G_WIDTH = 512
G_GROUP_DIM = G_WIDTH // G_GROUPS
EPS = 1e-6

LANES = 128
BF16_ROWS = 16
V_ROWS = A_HEAD_DIM + BF16_ROWS
HEAD_PAIRS = A_WIDTH // LANES
GROUP_PAIRS = G_WIDTH // LANES
MXU_COLS = 256
FF_CHUNKS = (768, 768, 768, 512)
assert sum(FF_CHUNKS) == D_FF and all(c % MXU_COLS == 0 for c in FF_CHUNKS)

TOKEN_TILE = 512
MIX_TILE = 1024
MIX_ROW_GROUPS = 4
SIDE_CAST_BLOCKS = 16
FIRST_CAST_BLOCKS = 8
VMEM_LIMIT_BYTES = 56 * 1024 * 1024
MASKED = -1e30
LOG2_E = 1.4426950408889634
TILE_PAIR = 2
LONG_RUN = 4
F32 = jnp.float32
BF16 = jnp.bfloat16


def _dot(a, b):
    return jnp.dot(a, b, preferred_element_type=F32)


def _rmsnorm(x, g):
    return x * lax.rsqrt(jnp.mean(x * x, axis=-1, keepdims=True) + EPS) * g


def _swiglu_residual(x, norm_g, wg_ref, wu_ref, wd_ref):
    n = _rmsnorm(x, norm_g).astype(BF16)
    acc = jnp.zeros(x.shape, F32)
    start = 0
    for width in FF_CHUNKS:
        gate = _dot(n, wg_ref[:, start:start + width])
        up = _dot(n, wu_ref[:, start:start + width])
        act = (gate * jax.nn.sigmoid(gate) * up).astype(BF16)
        acc = acc + _dot(act, wd_ref[start:start + width, :])
        start += width
    return x + 0.5 * acc


def _ffn1_qkv_kernel(x_ref, g1_ref, wg_ref, wu_ref, wd_ref, gmix_ref, wqkv_ref, *rest):
    k = (len(rest) - 4) // 2
    side_in, (h_ref, qkv_ref, kmean_ref), side_out = rest[:k + 1], rest[k + 1:k + 4], rest[k + 4:]
    for src, dst in zip(side_in[:-2], side_out[:-1]):
        dst[...] = src[...].astype(BF16)
    half = side_in[-1].shape[-1]
    side_out[-1][:, :half] = side_in[-2][...].astype(BF16)
    side_out[-1][:, half:] = side_in[-1][...].astype(BF16)
    h = _swiglu_residual(x_ref[...], g1_ref[...], wg_ref, wu_ref, wd_ref)
    h_ref[...] = h
    n = _rmsnorm(h, gmix_ref[...]).astype(BF16)
    z = _dot(n, wqkv_ref[...])
    qkv_ref[...] = z.astype(BF16)
    for j in range(TOKEN_TILE // MOBA_BLOCK):
        kb = z[j * MOBA_BLOCK:(j + 1) * MOBA_BLOCK, A_WIDTH:2 * A_WIDTH]
        kmean_ref[j] = jnp.mean(kb, axis=0, keepdims=True)


def _cast_kernel(*refs):
    n = len(refs) // 2
    for src, dst in zip(refs[:n], refs[n:]):
        dst[...] = src[...].astype(BF16)


def _ffn2_out_kernel(h_ref, g2_ref, wg_ref, wu_ref, wd_ref, gfin_ref, o_ref):
    h = _swiglu_residual(h_ref[...], g2_ref[...], wg_ref, wu_ref, wd_ref)
    o_ref[...] = _rmsnorm(h, gfin_ref[...])


def _moba_kernel(q_ref, k_ref, v_ref, kmean_ref, o_ref, vta_ref, qta_ref, s_ref, smax_ref, acc_ref):
    j = pl.program_id(1)
    seq = k_ref.shape[1]
    n_blocks = seq // MOBA_BLOCK
    scale = A_HEAD_DIM ** -0.5
    first_q = TILE_PAIR * j
    n_chains = TILE_PAIR * A_HEADS
    all_chains = list(range(n_chains))
    last_tile_chains = all_chains[-A_HEADS:]

    @pl.when(j == 0)
    def _():
        ones_row = (lax.broadcasted_iota(jnp.int32, (BF16_ROWS, seq), 0) == 0).astype(BF16)
        for h in range(A_HEADS):
            vta_ref[h * V_ROWS + A_HEAD_DIM:(h + 1) * V_ROWS, :] = ones_row
        for c in all_chains:
            qta_ref[c, LANES + n_blocks:, :] = jnp.zeros((LANES - n_blocks, MOBA_BLOCK), BF16)
        for n in range(n_blocks):
            blk = slice(n * MOBA_BLOCK, (n + 1) * MOBA_BLOCK)
            vt = v_ref[0, blk, :].astype(F32).T.astype(BF16)
            for h in range(A_HEADS):
                vta_ref[h * V_ROWS:h * V_ROWS + A_HEAD_DIM, blk] = vt[h * A_HEAD_DIM:(h + 1) * A_HEAD_DIM, :]

    lane = lax.broadcasted_iota(jnp.int32, (MOBA_BLOCK, LANES), 1)
    feat = lax.broadcasted_iota(jnp.int32, (LANES, MOBA_BLOCK), 0)
    blk_row = lax.broadcasted_iota(jnp.int32, (n_blocks, MOBA_BLOCK), 0)
    key_pos = lax.broadcasted_iota(jnp.int32, (MOBA_BLOCK, MOBA_BLOCK), 0)
    qry_pos = lax.broadcasted_iota(jnp.int32, (MOBA_BLOCK, MOBA_BLOCK), 1)

    def score_head(n, slot, c, own=False):
        pair = (c % A_HEADS) // 2
        start = pl.multiple_of(n * MOBA_BLOCK, MOBA_BLOCK)
        k_pair = k_ref[0, pl.ds(start, MOBA_BLOCK), pair * LANES:(pair + 1) * LANES]
        if own:
            s = jnp.where(key_pos <= qry_pos, _dot(k_pair, qta_ref[c, :LANES, :]), MASKED)
        else:
            s = _dot(jnp.concatenate([k_pair, (lane == n).astype(BF16)], axis=1), qta_ref[c])
        s_ref[slot, c] = s
        smax_ref[slot, c] = jnp.max(s, axis=0, keepdims=True)

    for tile in range(TILE_PAIR):
        qi = first_q + tile
        valid = blk_row < qi
        rows = slice(tile * MOBA_BLOCK, (tile + 1) * MOBA_BLOCK)
        for pair in range(HEAD_PAIRS):
            cols = slice(pair * LANES, (pair + 1) * LANES)
            qt = (q_ref[0, rows, cols].astype(F32) * scale).T
            kmean = kmean_ref[0, :, cols].astype(BF16)
            for half in range(2):
                c = tile * A_HEADS + 2 * pair + half
                in_head = (feat >= half * A_HEAD_DIM) & (feat < (half + 1) * A_HEAD_DIM)
                qth = jnp.where(in_head, qt, 0.0).astype(BF16)
                qta_ref[c, :LANES, :] = jnp.where(in_head, qt * LOG2_E, 0.0).astype(BF16)
                score_head(qi, 0, c, own=True)
                gate = jnp.where(valid, _dot(kmean, qth), -jnp.inf)
                selected = jnp.zeros(gate.shape, jnp.bool_)
                for _ in range(MOBA_TOPK):
                    best = jnp.max(gate, axis=0, keepdims=True)
                    first = jnp.min(jnp.where(gate == best, blk_row, n_blocks), axis=0, keepdims=True)
                    pick = blk_row == first
                    selected = selected | pick
                    gate = jnp.where(pick, -jnp.inf, gate)
                selected = selected & valid
                qta_ref[c, LANES:LANES + n_blocks, :] = jnp.where(selected, 0.0, MASKED).astype(BF16)

    def update_head(n, slot, c, m_prev):
        h = c % A_HEADS
        start = pl.multiple_of(n * MOBA_BLOCK, MOBA_BLOCK)
        m_new = smax_ref[slot, c]
        if m_prev is not None:
            m_new = jnp.maximum(m_prev, m_new)
        p = jnp.exp2(s_ref[slot, c] - m_new).astype(BF16)
        pv = _dot(vta_ref[h * V_ROWS:(h + 1) * V_ROWS, pl.ds(start, MOBA_BLOCK)], p)
        if m_prev is None:
            acc_ref[c] = pv
        else:
            acc_ref[c] = jnp.exp2(m_prev - m_new) * acc_ref[c] + pv
        return m_new

    def advance(n, slot, chains, m_run, n_next=None, chains_next=()):
        m_out = list(m_run) if m_run is not None else [None] * n_chains
        todo = list(chains_next)
        for c in chains:
            if todo:
                score_head(n_next, 1 - slot, todo.pop(0))
            m_out[c] = update_head(n(c) if callable(n) else n, slot, c, None if m_run is None else m_run[c])
        for c in todo:
            score_head(n_next, 1 - slot, c)
        return tuple(m_out)

    m_run = advance(lambda c: first_q + c // A_HEADS, 0, all_chains, None, 0, all_chains)

    def run_of_blocks(n0, count, m_run):
        for i in range(count):
            m_run = advance(n0 + i, (i + 1) % 2, all_chains, m_run, n0 + i + 1, all_chains)
        return m_run

    ahead = jnp.maximum(first_q - 1, 0)
    long_trips = lax.shift_right_logical(ahead, LONG_RUN.bit_length() - 1)
    pairs_start = long_trips * LONG_RUN
    pair_trips = lax.shift_right_logical(ahead - pairs_start, 1)
    single_start = pairs_start + 2 * pair_trips
    m_run = lax.fori_loop(0, long_trips, lambda i, m: run_of_blocks(i * LONG_RUN, LONG_RUN, m), m_run)
    m_run = lax.fori_loop(0, pair_trips, lambda i, m: run_of_blocks(pairs_start + 2 * i, 2, m), m_run)
    m_run = lax.fori_loop(0, ahead - single_start, lambda i, m: run_of_blocks(single_start, 1, m), m_run)
    m_run = lax.fori_loop(0, jnp.where(j > 0, 1, 0),
                          lambda i, m: advance(ahead, 0, all_chains, m, first_q, last_tile_chains), m_run)
    advance(first_q, 1, last_tile_chains, m_run)

    for tile in range(TILE_PAIR):
        rows = slice(tile * MOBA_BLOCK, (tile + 1) * MOBA_BLOCK)
        for pair in range(HEAD_PAIRS):
            outs = []
            for half in range(2):
                acc = acc_ref[tile * A_HEADS + 2 * pair + half]
                outs.append(acc[:A_HEAD_DIM] / acc[A_HEAD_DIM:A_HEAD_DIM + 1])
            o_ref[0, rows, pair * LANES:(pair + 1) * LANES] = jnp.concatenate(outs, axis=0).T.astype(o_ref.dtype)


def _mix_kernel(h_ref, attn_ref, gmix_ref, wuv_ref, lng_ref, lnb_ref, ws_ref, bs_ref,
                wba_ref, wbg_ref, wgate_ref, bgate_ref, wout_ref, o_ref):
    rows = h_ref.shape[0] // MIX_ROW_GROUPS
    n_chunks = rows // G_CHUNK
    row = lax.broadcasted_iota(jnp.int32, (G_CHUNK, G_CHUNK), 0)
    col = lax.broadcasted_iota(jnp.int32, (G_CHUNK, G_CHUNK), 1)
    lane = lax.broadcasted_iota(jnp.int32, (G_CHUNK, n_chunks * LANES), 1)
    low_half = (lane % LANES) < G_GROUP_DIM

    def input_matmuls(r):
        sl = slice(r * rows, (r + 1) * rows)
        h = h_ref[sl, :]
        n = _rmsnorm(h, gmix_ref[...]).astype(BF16)
        return sl, h, _dot(n, wuv_ref[...]), _dot(n, wgate_ref[...])

    staged = [input_matmuls(0)]
    for r in range(MIX_ROW_GROUPS):
        if r + 1 < MIX_ROW_GROUPS:
            staged.append(input_matmuls(r + 1))
        sl, h, z_uv, gate_logits = staged[r]
        zg = jax.nn.gelu(z_uv, approximate=True)
        u = zg[:, :G_WIDTH]
        v = zg[:, G_WIDTH:]
        mu = jnp.mean(v, axis=-1, keepdims=True)
        var = jnp.mean(jnp.square(v - mu), axis=-1, keepdims=True)
        v = ((v - mu) * lax.rsqrt(var + EPS) * lng_ref[...] + lnb_ref[...]).astype(BF16)
        mixed_cols = []
        for p in range(GROUP_PAIRS):
            vp = jnp.concatenate(
                [v[c * G_CHUNK:(c + 1) * G_CHUNK, p * LANES:(p + 1) * LANES] for c in range(n_chunks)], axis=1)
            zeros = jnp.zeros_like(vp)
            rhs = jnp.concatenate([jnp.where(low_half, vp, zeros), jnp.where(low_half, zeros, vp)], axis=0)
            w_lo = jnp.where(col <= row, ws_ref[2 * p], 0.0).astype(BF16)
            w_hi = jnp.where(col <= row, ws_ref[2 * p + 1], 0.0).astype(BF16)
            mp = _dot(jnp.concatenate([w_lo, w_hi], axis=1), rhs)
            mixed_cols.append(jnp.concatenate(
                [mp[:, c * LANES:(c + 1) * LANES] for c in range(n_chunks)], axis=0))
        mixed = jnp.concatenate(mixed_cols, axis=1)
        bias = jnp.concatenate([bs_ref[...]] * n_chunks, axis=0)
        gm = (u * (mixed + bias)).astype(BF16)

        y_attn = _dot(attn_ref[sl, :], wba_ref[...])
        y_gmlp = _dot(gm, wbg_ref[...])
        gates = jax.nn.sigmoid(gate_logits + bgate_ref[...])
        merged = gates[:, :D_MODEL] * y_attn + gates[:, D_MODEL:] * y_gmlp
        o_ref[sl, :] = h + _dot(merged.astype(BF16), wout_ref[...])


def _resident(shape):
    zeros = (0,) * len(shape)
    return pl.BlockSpec(shape, lambda *_: zeros, pipeline_mode=pl.Buffered(1))


def _row_tiles(width, rows=TOKEN_TILE):
    return pl.BlockSpec((rows, width), lambda i: (i, 0))


def _dense_params():
    return pltpu.CompilerParams(dimension_semantics=("parallel",), vmem_limit_bytes=VMEM_LIMIT_BYTES)


def kernel(x, ffn1_norm, ffn1_w_gate, ffn1_w_up, ffn1_w_down, mix_norm, w_in, gmlp_ln_g, gmlp_ln_b, gmlp_w_s, gmlp_b_s, w_branch_attn, w_branch_gmlp, w_gate, b_gate, w_out, ffn2_norm, ffn2_w_gate, ffn2_w_up, ffn2_w_down, final_norm):
    B, S, D = x.shape
    T = B * S
    n_blocks = S // MOBA_BLOCK
    assert D == D_MODEL and S % MOBA_BLOCK == 0 and T % TOKEN_TILE == 0
    assert TOKEN_TILE % MOBA_BLOCK == 0 and TOKEN_TILE % G_CHUNK == 0
    assert n_blocks <= LANES, "one one-hot lane per key block"
    assert ffn1_norm.shape[0] == 1, "single-layer block"
    vec = lambda w: w.reshape(1, -1).astype(F32)
    n_tiles = T // TOKEN_TILE
    x2 = x.reshape(T, D)

    first = [(ffn1_w_gate, D_FF), (ffn1_w_up, D_FF), (ffn1_w_down, D), (w_in, 3 * A_WIDTH)]
    assert all(w.shape[1] % (FIRST_CAST_BLOCKS * BF16_ROWS) == 0 for w, _ in first)
    ffn1_wg, ffn1_wu, ffn1_wd, w_qkv = pl.pallas_call(
        _cast_kernel,
        grid=(FIRST_CAST_BLOCKS,),
        in_specs=[pl.BlockSpec((None, w.shape[1] // FIRST_CAST_BLOCKS, cols), lambda i: (0, i, 0))
                  for w, cols in first],
        out_specs=[pl.BlockSpec((w.shape[1] // FIRST_CAST_BLOCKS, cols), lambda i: (i, 0)) for w, cols in first],
        out_shape=[jax.ShapeDtypeStruct((w.shape[1], cols), BF16) for w, cols in first],
        compiler_params=_dense_params(),
        name="cast_first",
    )(*[w for w, _ in first])

    later = [ffn2_w_gate, ffn2_w_up, ffn2_w_down, w_gate, w_out, w_branch_attn, w_branch_gmlp]
    steps_per_block, rest = divmod(n_tiles, SIDE_CAST_BLOCKS)
    assert rest == 0 and all(w.shape[1] % (SIDE_CAST_BLOCKS * BF16_ROWS) == 0 for w in later + [w_in])
    row_block = lambda i: i // steps_per_block
    side_in_specs = [pl.BlockSpec((None, w.shape[1] // SIDE_CAST_BLOCKS, w.shape[2]), lambda i: (0, row_block(i), 0))
                     for w in later]
    side_out_specs = [pl.BlockSpec((w.shape[1] // SIDE_CAST_BLOCKS, w.shape[2]), lambda i: (row_block(i), 0))
                      for w in later]
    side_out_shapes = [jax.ShapeDtypeStruct(w.shape[1:], BF16) for w in later]
    assert (3 * A_WIDTH) % G_WIDTH == 0
    first_g = 3 * A_WIDTH // G_WIDTH
    side_in_specs += [pl.BlockSpec((None, D // SIDE_CAST_BLOCKS, G_WIDTH), lambda i, c=c: (0, row_block(i), first_g + c))
                      for c in range(2)]
    side_out_specs.append(pl.BlockSpec((D // SIDE_CAST_BLOCKS, 2 * G_WIDTH), lambda i: (row_block(i), 0)))
    side_out_shapes.append(jax.ShapeDtypeStruct((D, 2 * G_WIDTH), BF16))
    h1, qkv, kmean, *later_bf16 = pl.pallas_call(
        _ffn1_qkv_kernel,
        grid=(n_tiles,),
        in_specs=[_row_tiles(D), _resident((1, D)), _resident((D, D_FF)), _resident((D, D_FF)),
                  _resident((D_FF, D)), _resident((1, D)), _resident((D, 3 * A_WIDTH))] + side_in_specs,
        out_specs=[_row_tiles(D), _row_tiles(3 * A_WIDTH),
                   pl.BlockSpec((TOKEN_TILE // MOBA_BLOCK, 1, A_WIDTH), lambda i: (i, 0, 0))] + side_out_specs,
        out_shape=[jax.ShapeDtypeStruct((T, D), F32), jax.ShapeDtypeStruct((T, 3 * A_WIDTH), BF16),
                   jax.ShapeDtypeStruct((T // MOBA_BLOCK, 1, A_WIDTH), F32)] + side_out_shapes,
        compiler_params=pltpu.CompilerParams(dimension_semantics=("arbitrary",),
                                             vmem_limit_bytes=VMEM_LIMIT_BYTES),
        name="ffn1_qkv",
    )(x2, vec(ffn1_norm), ffn1_wg, ffn1_wu, ffn1_wd, vec(mix_norm), w_qkv, *later, w_in, w_in)
    ffn2_wg, ffn2_wu, ffn2_wd, w_gate_bf, w_out_bf, w_ba_bf, w_bg_bf, w_uv_bf = later_bf16

    qkv3 = qkv.reshape(B, S, 3 * A_WIDTH)
    q_rows = TILE_PAIR * MOBA_BLOCK
    n_chains = TILE_PAIR * A_HEADS
    assert TILE_PAIR == 2 and n_blocks % TILE_PAIR == 0
    attn = pl.pallas_call(
        _moba_kernel,
        grid=(B, n_blocks // TILE_PAIR),
        in_specs=[pl.BlockSpec((1, q_rows, A_WIDTH), lambda b, i: (b, i, 0)),
                  pl.BlockSpec((1, S, A_WIDTH), lambda b, i: (b, 0, 1)),
                  pl.BlockSpec((1, S, A_WIDTH), lambda b, i: (b, 0, 2)),
                  pl.BlockSpec((1, n_blocks, A_WIDTH), lambda b, i: (b, 0, 0))],
        out_specs=pl.BlockSpec((1, q_rows, A_WIDTH), lambda b, i: (b, i, 0)),
        out_shape=jax.ShapeDtypeStruct((B, S, A_WIDTH), BF16),
        scratch_shapes=[pltpu.VMEM((A_HEADS * V_ROWS, S), BF16),
                        pltpu.VMEM((n_chains, 2 * LANES, MOBA_BLOCK), BF16),
                        pltpu.VMEM((2, n_chains, MOBA_BLOCK, MOBA_BLOCK), F32),
                        pltpu.VMEM((2, n_chains, 1, MOBA_BLOCK), F32),
                        pltpu.VMEM((n_chains, V_ROWS, MOBA_BLOCK), F32)],
        compiler_params=pltpu.CompilerParams(dimension_semantics=("parallel", "arbitrary"),
                                             vmem_limit_bytes=VMEM_LIMIT_BYTES),
        name="moba_attn",
    )(qkv3, qkv3, qkv3, kmean.reshape(B, n_blocks, A_WIDTH))

    bs_lanes = jnp.repeat(gmlp_b_s[0].T, G_GROUP_DIM, axis=1).astype(F32)
    h2 = pl.pallas_call(
        _mix_kernel,
        grid=(T // MIX_TILE,),
        in_specs=[_row_tiles(D, MIX_TILE), _row_tiles(A_WIDTH, MIX_TILE), _resident((1, D)), _resident((D, 2 * G_WIDTH)),
                  _resident((1, G_WIDTH)), _resident((1, G_WIDTH)),
                  _resident((G_GROUPS, G_CHUNK, G_CHUNK)), _resident((G_CHUNK, G_WIDTH)),
                  _resident((A_WIDTH, D)), _resident((G_WIDTH, D)), _resident((D, 2 * D)),
                  _resident((1, 2 * D)), _resident((D, D))],
        out_specs=_row_tiles(D, MIX_TILE),
        out_shape=jax.ShapeDtypeStruct((T, D), F32),
        compiler_params=_dense_params(),
        name="mix",
    )(h1, attn.reshape(T, A_WIDTH), vec(mix_norm), w_uv_bf, vec(gmlp_ln_g),
      vec(gmlp_ln_b), gmlp_w_s[0].astype(F32), bs_lanes, w_ba_bf, w_bg_bf,
      w_gate_bf, vec(b_gate), w_out_bf)

    out = pl.pallas_call(
        _ffn2_out_kernel,
        grid=(n_tiles,),
        in_specs=[_row_tiles(D), _resident((1, D)), _resident((D, D_FF)), _resident((D, D_FF)),
                  _resident((D_FF, D)), _resident((1, D))],
        out_specs=_row_tiles(D),
        out_shape=jax.ShapeDtypeStruct((T, D), F32),
        compiler_params=_dense_params(),
        name="ffn2_out",
    )(h2, vec(ffn2_norm), ffn2_wg, ffn2_wu, ffn2_wd, vec(final_norm))
    return out.reshape(B, S, D)
```

```python
import jax
import jax.numpy as jnp
from jax import lax
from jax.experimental import pallas as pl
from jax.experimental.pallas import tpu as pltpu

D_MODEL = 1024
D_FF = 2816
A_HEADS = 8
A_HEAD_DIM = 64
A_WIDTH = A_HEADS * A_HEAD_DIM
MOBA_BLOCK = 256
MOBA_TOPK = 3
G_GROUPS = 8
G_CHUNK = 128
G_WIDTH = 512
G_GROUP_DIM = G_WIDTH // G_GROUPS
EPS = 1e-6

LANES = 128
BF16_ROWS = 16
V_ROWS = A_HEAD_DIM + BF16_ROWS
HEAD_PAIRS = A_WIDTH // LANES
GROUP_PAIRS = G_WIDTH // LANES
MXU_COLS = 256
FF_CHUNKS = (768, 768, 768, 512)
assert sum(FF_CHUNKS) == D_FF and all(c % MXU_COLS == 0 for c in FF_CHUNKS)

TOKEN_TILE = 512
FFN2_TILE = 1024
FFN2_ROW_GROUPS = 4
MIX_TILE = 1024
MIX_ROW_GROUPS = 4
SIDE_CAST_BLOCKS = 16
FIRST_CAST_BLOCKS = 8
VMEM_LIMIT_BYTES = 56 * 1024 * 1024
MASKED = -1e30
LOG2_E = 1.4426950408889634
TILE_PAIR = 2
LONG_RUN = 4
F32 = jnp.float32
BF16 = jnp.bfloat16


def _dot(a, b):
    return jnp.dot(a, b, preferred_element_type=F32)


def _rmsnorm(x, g):
    return x * lax.rsqrt(jnp.mean(x * x, axis=-1, keepdims=True) + EPS) * g


def _swiglu_residual(x, norm_g, wg_ref, wu_ref, wd_ref):
    out = []
    _swiglu_residual_groups([x], norm_g, wg_ref, wu_ref, wd_ref, lambda g, h: out.append(h))
    return out[0]


def _swiglu_residual_groups(xs, norm_g, wg_ref, wu_ref, wd_ref, finish):
    bounds, start = [], 0
    for width in FF_CHUNKS:
        bounds.append((start, width))
        start += width
    steps = [(g, ci) for g in range(len(xs)) for ci in range(len(bounds))]
    normed = {}

    def gate_up(g, ci):
        if g not in normed:
            normed[g] = _rmsnorm(xs[g], norm_g).astype(BF16)
        lo, width = bounds[ci]
        return _dot(normed[g], wg_ref[:, lo:lo + width]), _dot(normed[g], wu_ref[:, lo:lo + width])

    acc = None
    ahead = gate_up(*steps[0])
    for t, (g, ci) in enumerate(steps):
        gate, up = ahead
        if t + 1 < len(steps):
            ahead = gate_up(*steps[t + 1])
        lo, width = bounds[ci]
        down = _dot((gate * jax.nn.sigmoid(gate) * up).astype(BF16), wd_ref[lo:lo + width, :])
        acc = down if ci == 0 else acc + down
        if ci == len(bounds) - 1:
            finish(g, xs[g] + 0.5 * acc)


def _ffn1_qkv_kernel(x_ref, g1_ref, wg_ref, wu_ref, wd_ref, gmix_ref, wqkv_ref, *rest):
    k = (len(rest) - 4) // 2
    side_in, (h_ref, qkv_ref, kmean_ref), side_out = rest[:k + 1], rest[k + 1:k + 4], rest[k + 4:]
    for src, dst in zip(side_in[:-2], side_out[:-1]):
        dst[...] = src[...].astype(BF16)
    half = side_in[-1].shape[-1]
    side_out[-1][:, :half] = side_in[-2][...].astype(BF16)
    side_out[-1][:, half:] = side_in[-1][...].astype(BF16)
    h = _swiglu_residual(x_ref[...], g1_ref[...], wg_ref, wu_ref, wd_ref)
    h_ref[...] = h
    n = _rmsnorm(h, gmix_ref[...]).astype(BF16)
    z = _dot(n, wqkv_ref[...])
    qkv_ref[...] = z.astype(BF16)
    for j in range(TOKEN_TILE // MOBA_BLOCK):
        kb = z[j * MOBA_BLOCK:(j + 1) * MOBA_BLOCK, A_WIDTH:2 * A_WIDTH]
        kmean_ref[j] = jnp.mean(kb, axis=0, keepdims=True)


def _cast_kernel(*refs):
    n = len(refs) // 2
    for src, dst in zip(refs[:n], refs[n:]):
        dst[...] = src[...].astype(BF16)


def _ffn2_out_kernel(h_ref, g2_ref, wg_ref, wu_ref, wd_ref, gfin_ref, o_ref):
    rows = h_ref.shape[0] // FFN2_ROW_GROUPS
    groups = [slice(g * rows, (g + 1) * rows) for g in range(FFN2_ROW_GROUPS)]

    def finish(g, h):
        o_ref[groups[g], :] = _rmsnorm(h, gfin_ref[...])

    _swiglu_residual_groups([h_ref[sl, :] for sl in groups], g2_ref[...], wg_ref, wu_ref, wd_ref, finish)


def _moba_kernel(q_ref, k_ref, v_ref, kmean_ref, o_ref, vta_ref, qta_ref, s_ref, smax_ref, acc_ref):
    j = pl.program_id(1)
    seq = k_ref.shape[1]
    n_blocks = seq // MOBA_BLOCK
    scale = A_HEAD_DIM ** -0.5
    first_q = TILE_PAIR * j
    n_chains = TILE_PAIR * A_HEADS
    all_chains = list(range(n_chains))
    last_tile_chains = all_chains[-A_HEADS:]

    @pl.when(j == 0)
    def _():
        ones_row = (lax.broadcasted_iota(jnp.int32, (BF16_ROWS, seq), 0) == 0).astype(BF16)
        for h in range(A_HEADS):
            vta_ref[h * V_ROWS + A_HEAD_DIM:(h + 1) * V_ROWS, :] = ones_row
        for c in all_chains:
            qta_ref[c, LANES + n_blocks:, :] = jnp.zeros((LANES - n_blocks, MOBA_BLOCK), BF16)
        for n in range(n_blocks):
            blk = slice(n * MOBA_BLOCK, (n + 1) * MOBA_BLOCK)
            vt = v_ref[0, blk, :].astype(F32).T.astype(BF16)
            for h in range(A_HEADS):
                vta_ref[h * V_ROWS:h * V_ROWS + A_HEAD_DIM, blk] = vt[h * A_HEAD_DIM:(h + 1) * A_HEAD_DIM, :]

    lane = lax.broadcasted_iota(jnp.int32, (MOBA_BLOCK, LANES), 1)
    feat = lax.broadcasted_iota(jnp.int32, (LANES, MOBA_BLOCK), 0)
    blk_row = lax.broadcasted_iota(jnp.int32, (n_blocks, MOBA_BLOCK), 0)
    key_pos = lax.broadcasted_iota(jnp.int32, (MOBA_BLOCK, MOBA_BLOCK), 0)
    qry_pos = lax.broadcasted_iota(jnp.int32, (MOBA_BLOCK, MOBA_BLOCK), 1)

    def score_head(n, slot, c, own=False):
        pair = (c % A_HEADS) // 2
        start = pl.multiple_of(n * MOBA_BLOCK, MOBA_BLOCK)
        k_pair = k_ref[0, pl.ds(start, MOBA_BLOCK), pair * LANES:(pair + 1) * LANES]
        if own:
            s = jnp.where(key_pos <= qry_pos, _dot(k_pair, qta_ref[c, :LANES, :]), MASKED)
        else:
            s = _dot(jnp.concatenate([k_pair, (lane == n).astype(BF16)], axis=1), qta_ref[c])
        s_ref[slot, c] = s
        smax_ref[slot, c] = jnp.max(s, axis=0, keepdims=True)

    for tile in range(TILE_PAIR):
        qi = first_q + tile
        valid = blk_row < qi
        rows = slice(tile * MOBA_BLOCK, (tile + 1) * MOBA_BLOCK)
        for pair in range(HEAD_PAIRS):
            cols = slice(pair * LANES, (pair + 1) * LANES)
            qt = (q_ref[0, rows, cols].astype(F32) * scale).T
            kmean = kmean_ref[0, :, cols].astype(BF16)
            for half in range(2):
                c = tile * A_HEADS + 2 * pair + half
                in_head = (feat >= half * A_HEAD_DIM) & (feat < (half + 1) * A_HEAD_DIM)
                qth = jnp.where(in_head, qt, 0.0).astype(BF16)
                qta_ref[c, :LANES, :] = jnp.where(in_head, qt * LOG2_E, 0.0).astype(BF16)
                score_head(qi, 0, c, own=True)
                gate = jnp.where(valid, _dot(kmean, qth), -jnp.inf)
                selected = jnp.zeros(gate.shape, jnp.bool_)
                for _ in range(MOBA_TOPK):
                    best = jnp.max(gate, axis=0, keepdims=True)
                    first = jnp.min(jnp.where(gate == best, blk_row, n_blocks), axis=0, keepdims=True)
                    pick = blk_row == first
                    selected = selected | pick
                    gate = jnp.where(pick, -jnp.inf, gate)
                selected = selected & valid
                qta_ref[c, LANES:LANES + n_blocks, :] = jnp.where(selected, 0.0, MASKED).astype(BF16)

    def update_head(n, slot, c, m_prev):
        h = c % A_HEADS
        start = pl.multiple_of(n * MOBA_BLOCK, MOBA_BLOCK)
        m_new = smax_ref[slot, c]
        if m_prev is not None:
            m_new = jnp.maximum(m_prev, m_new)
        p = jnp.exp2(s_ref[slot, c] - m_new).astype(BF16)
        pv = _dot(vta_ref[h * V_ROWS:(h + 1) * V_ROWS, pl.ds(start, MOBA_BLOCK)], p)
        if m_prev is None:
            acc_ref[c] = pv
        else:
            acc_ref[c] = jnp.exp2(m_prev - m_new) * acc_ref[c] + pv
        return m_new

    def advance(n, slot, chains, m_run, n_next=None, chains_next=()):
        m_out = list(m_run) if m_run is not None else [None] * n_chains
        todo = list(chains_next)
        for c in chains:
            if todo:
                score_head(n_next, 1 - slot, todo.pop(0))
            m_out[c] = update_head(n(c) if callable(n) else n, slot, c, None if m_run is None else m_run[c])
        for c in todo:
            score_head(n_next, 1 - slot, c)
        return tuple(m_out)

    m_run = advance(lambda c: first_q + c // A_HEADS, 0, all_chains, None, 0, all_chains)

    def run_of_blocks(n0, count, m_run):
        for i in range(count):
            m_run = advance(n0 + i, (i + 1) % 2, all_chains, m_run, n0 + i + 1, all_chains)
        return m_run

    ahead = jnp.maximum(first_q - 1, 0)
    long_trips = lax.shift_right_logical(ahead, LONG_RUN.bit_length() - 1)
    pairs_start = long_trips * LONG_RUN
    pair_trips = lax.shift_right_logical(ahead - pairs_start, 1)
    single_start = pairs_start + 2 * pair_trips
    m_run = lax.fori_loop(0, long_trips, lambda i, m: run_of_blocks(i * LONG_RUN, LONG_RUN, m), m_run)
    m_run = lax.fori_loop(0, pair_trips, lambda i, m: run_of_blocks(pairs_start + 2 * i, 2, m), m_run)
    m_run = lax.fori_loop(0, ahead - single_start, lambda i, m: run_of_blocks(single_start, 1, m), m_run)
    m_run = lax.fori_loop(0, jnp.where(j > 0, 1, 0),
                          lambda i, m: advance(ahead, 0, all_chains, m, first_q, last_tile_chains), m_run)
    advance(first_q, 1, last_tile_chains, m_run)

    for tile in range(TILE_PAIR):
        rows = slice(tile * MOBA_BLOCK, (tile + 1) * MOBA_BLOCK)
        for pair in range(HEAD_PAIRS):
            outs = []
            for half in range(2):
                acc = acc_ref[tile * A_HEADS + 2 * pair + half]
                outs.append(acc[:A_HEAD_DIM] / acc[A_HEAD_DIM:A_HEAD_DIM + 1])
            o_ref[0, rows, pair * LANES:(pair + 1) * LANES] = jnp.concatenate(outs, axis=0).T.astype(o_ref.dtype)


def _mix_kernel(h_ref, attn_ref, gmix_ref, wuv_ref, lng_ref, lnb_ref, ws_ref, bs_ref,
                wba_ref, wbg_ref, wgate_ref, bgate_ref, wout_ref, o_ref):
    rows = h_ref.shape[0] // MIX_ROW_GROUPS
    n_chunks = rows // G_CHUNK
    row = lax.broadcasted_iota(jnp.int32, (G_CHUNK, G_CHUNK), 0)
    col = lax.broadcasted_iota(jnp.int32, (G_CHUNK, G_CHUNK), 1)
    lane = lax.broadcasted_iota(jnp.int32, (G_CHUNK, n_chunks * LANES), 1)
    low_half = (lane % LANES) < G_GROUP_DIM

    def input_matmuls(r):
        sl = slice(r * rows, (r + 1) * rows)
        h = h_ref[sl, :]
        n = _rmsnorm(h, gmix_ref[...]).astype(BF16)
        return sl, h, _dot(n, wuv_ref[...]), _dot(n, wgate_ref[...])

    staged = [input_matmuls(0)]
    for r in range(MIX_ROW_GROUPS):
        if r + 1 < MIX_ROW_GROUPS:
            staged.append(input_matmuls(r + 1))
        sl, h, z_uv, gate_logits = staged[r]
        zg = jax.nn.gelu(z_uv, approximate=True)
        u = zg[:, :G_WIDTH]
        v = zg[:, G_WIDTH:]
        mu = jnp.mean(v, axis=-1, keepdims=True)
        var = jnp.mean(jnp.square(v - mu), axis=-1, keepdims=True)
        v = ((v - mu) * lax.rsqrt(var + EPS) * lng_ref[...] + lnb_ref[...]).astype(BF16)
        mixed_cols = []
        for p in range(GROUP_PAIRS):
            vp = jnp.concatenate(
                [v[c * G_CHUNK:(c + 1) * G_CHUNK, p * LANES:(p + 1) * LANES] for c in range(n_chunks)], axis=1)
            zeros = jnp.zeros_like(vp)
            rhs = jnp.concatenate([jnp.where(low_half, vp, zeros), jnp.where(low_half, zeros, vp)], axis=0)
            w_lo = jnp.where(col <= row, ws_ref[2 * p], 0.0).astype(BF16)
            w_hi = jnp.where(col <= row, ws_ref[2 * p + 1], 0.0).astype(BF16)
            mp = _dot(jnp.concatenate([w_lo, w_hi], axis=1), rhs)
            mixed_cols.append(jnp.concatenate(
                [mp[:, c * LANES:(c + 1) * LANES] for c in range(n_chunks)], axis=0))
        mixed = jnp.concatenate(mixed_cols, axis=1)
        bias = jnp.concatenate([bs_ref[...]] * n_chunks, axis=0)
        gm = (u * (mixed + bias)).astype(BF16)

        y_attn = _dot(attn_ref[sl, :], wba_ref[...])
        y_gmlp = _dot(gm, wbg_ref[...])
        gates = jax.nn.sigmoid(gate_logits + bgate_ref[...])
        merged = gates[:, :D_MODEL] * y_attn + gates[:, D_MODEL:] * y_gmlp
        o_ref[sl, :] = h + _dot(merged.astype(BF16), wout_ref[...])


def _resident(shape):
    zeros = (0,) * len(shape)
    return pl.BlockSpec(shape, lambda *_: zeros, pipeline_mode=pl.Buffered(1))


def _row_tiles(width, rows=TOKEN_TILE):
    return pl.BlockSpec((rows, width), lambda i: (i, 0))


def _dense_params():
    return pltpu.CompilerParams(dimension_semantics=("parallel",), vmem_limit_bytes=VMEM_LIMIT_BYTES)


def kernel(x, ffn1_norm, ffn1_w_gate, ffn1_w_up, ffn1_w_down, mix_norm, w_in, gmlp_ln_g, gmlp_ln_b, gmlp_w_s, gmlp_b_s, w_branch_attn, w_branch_gmlp, w_gate, b_gate, w_out, ffn2_norm, ffn2_w_gate, ffn2_w_up, ffn2_w_down, final_norm):
    B, S, D = x.shape
    T = B * S
    n_blocks = S // MOBA_BLOCK
    assert D == D_MODEL and S % MOBA_BLOCK == 0 and T % TOKEN_TILE == 0
    assert TOKEN_TILE % MOBA_BLOCK == 0 and TOKEN_TILE % G_CHUNK == 0
    assert n_blocks <= LANES, "one one-hot lane per key block"
    assert ffn1_norm.shape[0] == 1, "single-layer block"
    vec = lambda w: w.reshape(1, -1).astype(F32)
    n_tiles = T // TOKEN_TILE
    x2 = x.reshape(T, D)

    first = [(ffn1_w_gate, D_FF), (ffn1_w_up, D_FF), (ffn1_w_down, D), (w_in, 3 * A_WIDTH)]
    assert all(w.shape[1] % (FIRST_CAST_BLOCKS * BF16_ROWS) == 0 for w, _ in first)
    ffn1_wg, ffn1_wu, ffn1_wd, w_qkv = pl.pallas_call(
        _cast_kernel,
        grid=(FIRST_CAST_BLOCKS,),
        in_specs=[pl.BlockSpec((None, w.shape[1] // FIRST_CAST_BLOCKS, cols), lambda i: (0, i, 0))
                  for w, cols in first],
        out_specs=[pl.BlockSpec((w.shape[1] // FIRST_CAST_BLOCKS, cols), lambda i: (i, 0)) for w, cols in first],
        out_shape=[jax.ShapeDtypeStruct((w.shape[1], cols), BF16) for w, cols in first],
        compiler_params=_dense_params(),
        name="cast_first",
    )(*[w for w, _ in first])

    later = [ffn2_w_gate, ffn2_w_up, ffn2_w_down, w_gate, w_out, w_branch_attn, w_branch_gmlp]
    steps_per_block, rest = divmod(n_tiles, SIDE_CAST_BLOCKS)
    assert rest == 0 and all(w.shape[1] % (SIDE_CAST_BLOCKS * BF16_ROWS) == 0 for w in later + [w_in])
    row_block = lambda i: i // steps_per_block
    side_in_specs = [pl.BlockSpec((None, w.shape[1] // SIDE_CAST_BLOCKS, w.shape[2]), lambda i: (0, row_block(i), 0))
                     for w in later]
    side_out_specs = [pl.BlockSpec((w.shape[1] // SIDE_CAST_BLOCKS, w.shape[2]), lambda i: (row_block(i), 0))
                      for w in later]
    side_out_shapes = [jax.ShapeDtypeStruct(w.shape[1:], BF16) for w in later]
    assert (3 * A_WIDTH) % G_WIDTH == 0
    first_g = 3 * A_WIDTH // G_WIDTH
    side_in_specs += [pl.BlockSpec((None, D // SIDE_CAST_BLOCKS, G_WIDTH), lambda i, c=c: (0, row_block(i), first_g + c))
                      for c in range(2)]
    side_out_specs.append(pl.BlockSpec((D // SIDE_CAST_BLOCKS, 2 * G_WIDTH), lambda i: (row_block(i), 0)))
    side_out_shapes.append(jax.ShapeDtypeStruct((D, 2 * G_WIDTH), BF16))
    h1, qkv, kmean, *later_bf16 = pl.pallas_call(
        _ffn1_qkv_kernel,
        grid=(n_tiles,),
        in_specs=[_row_tiles(D), _resident((1, D)), _resident((D, D_FF)), _resident((D, D_FF)),
                  _resident((D_FF, D)), _resident((1, D)), _resident((D, 3 * A_WIDTH))] + side_in_specs,
        out_specs=[_row_tiles(D), _row_tiles(3 * A_WIDTH),
                   pl.BlockSpec((TOKEN_TILE // MOBA_BLOCK, 1, A_WIDTH), lambda i: (i, 0, 0))] + side_out_specs,
        out_shape=[jax.ShapeDtypeStruct((T, D), F32), jax.ShapeDtypeStruct((T, 3 * A_WIDTH), BF16),
                   jax.ShapeDtypeStruct((T // MOBA_BLOCK, 1, A_WIDTH), F32)] + side_out_shapes,
        compiler_params=pltpu.CompilerParams(dimension_semantics=("arbitrary",),
                                             vmem_limit_bytes=VMEM_LIMIT_BYTES),
        name="ffn1_qkv",
    )(x2, vec(ffn1_norm), ffn1_wg, ffn1_wu, ffn1_wd, vec(mix_norm), w_qkv, *later, w_in, w_in)
    ffn2_wg, ffn2_wu, ffn2_wd, w_gate_bf, w_out_bf, w_ba_bf, w_bg_bf, w_uv_bf = later_bf16

    qkv3 = qkv.reshape(B, S, 3 * A_WIDTH)
    q_rows = TILE_PAIR * MOBA_BLOCK
    n_chains = TILE_PAIR * A_HEADS
    assert TILE_PAIR == 2 and n_blocks % TILE_PAIR == 0
    attn = pl.pallas_call(
        _moba_kernel,
        grid=(B, n_blocks // TILE_PAIR),
        in_specs=[pl.BlockSpec((1, q_rows, A_WIDTH), lambda b, i: (b, i, 0)),
                  pl.BlockSpec((1, S, A_WIDTH), lambda b, i: (b, 0, 1)),
                  pl.BlockSpec((1, S, A_WIDTH), lambda b, i: (b, 0, 2)),
                  pl.BlockSpec((1, n_blocks, A_WIDTH), lambda b, i: (b, 0, 0))],
        out_specs=pl.BlockSpec((1, q_rows, A_WIDTH), lambda b, i: (b, i, 0)),
        out_shape=jax.ShapeDtypeStruct((B, S, A_WIDTH), BF16),
        scratch_shapes=[pltpu.VMEM((A_HEADS * V_ROWS, S), BF16),
                        pltpu.VMEM((n_chains, 2 * LANES, MOBA_BLOCK), BF16),
                        pltpu.VMEM((2, n_chains, MOBA_BLOCK, MOBA_BLOCK), F32),
                        pltpu.VMEM((2, n_chains, 1, MOBA_BLOCK), F32),
                        pltpu.VMEM((n_chains, V_ROWS, MOBA_BLOCK), F32)],
        compiler_params=pltpu.CompilerParams(dimension_semantics=("parallel", "arbitrary"),
                                             vmem_limit_bytes=VMEM_LIMIT_BYTES),
        name="moba_attn",
    )(qkv3, qkv3, qkv3, kmean.reshape(B, n_blocks, A_WIDTH))

    bs_lanes = jnp.repeat(gmlp_b_s[0].T, G_GROUP_DIM, axis=1).astype(F32)
    h2 = pl.pallas_call(
        _mix_kernel,
        grid=(T // MIX_TILE,),
        in_specs=[_row_tiles(D, MIX_TILE), _row_tiles(A_WIDTH, MIX_TILE), _resident((1, D)), _resident((D, 2 * G_WIDTH)),
                  _resident((1, G_WIDTH)), _resident((1, G_WIDTH)),
                  _resident((G_GROUPS, G_CHUNK, G_CHUNK)), _resident((G_CHUNK, G_WIDTH)),
                  _resident((A_WIDTH, D)), _resident((G_WIDTH, D)), _resident((D, 2 * D)),
                  _resident((1, 2 * D)), _resident((D, D))],
        out_specs=_row_tiles(D, MIX_TILE),
        out_shape=jax.ShapeDtypeStruct((T, D), F32),
        compiler_params=_dense_params(),
        name="mix",
    )(h1, attn.reshape(T, A_WIDTH), vec(mix_norm), w_uv_bf, vec(gmlp_ln_g),
      vec(gmlp_ln_b), gmlp_w_s[0].astype(F32), bs_lanes, w_ba_bf, w_bg_bf,
      w_gate_bf, vec(b_gate), w_out_bf)

    out = pl.pallas_call(
        _ffn2_out_kernel,
        grid=(T // FFN2_TILE,),
        in_specs=[_row_tiles(D, FFN2_TILE), _resident((1, D)), _resident((D, D_FF)), _resident((D, D_FF)),
                  _resident((D_FF, D)), _resident((1, D))],
        out_specs=_row_tiles(D, FFN2_TILE),
        out_shape=jax.ShapeDtypeStruct((T, D), F32),
        compiler_params=_dense_params(),
        name="ffn2_out",
    )(h2, vec(ffn2_norm), ffn2_wg, ffn2_wu, ffn2_wd, vec(final_norm))
    return out.reshape(B, S, D)
```

```python
import jax
import jax.numpy as jnp
from jax import lax
from jax.experimental import pallas as pl
from jax.experimental.pallas import tpu as pltpu

D_MODEL = 1024
D_FF = 2816
A_HEADS = 8
A_HEAD_DIM = 64
A_WIDTH = A_HEADS * A_HEAD_DIM
MOBA_BLOCK = 256
MOBA_TOPK = 3
G_GROUPS = 8
G_CHUNK = 128
G_WIDTH = 512
G_GROUP_DIM = G_WIDTH // G_GROUPS
EPS = 1e-6

LANES = 128
BF16_ROWS = 16
V_ROWS = A_HEAD_DIM + BF16_ROWS
HEAD_PAIRS = A_WIDTH // LANES
GROUP_PAIRS = G_WIDTH // LANES
MXU_COLS = 256
FF_CHUNKS = (768, 768, 768, 512)
assert sum(FF_CHUNKS) == D_FF and all(c % MXU_COLS == 0 for c in FF_CHUNKS)

TOKEN_TILE = 512
FFN2_TILE = 1024
FFN2_ROW_GROUPS = 4
MIX_TILE = 1024
MIX_ROW_GROUPS = 4
SIDE_CAST_BLOCKS = 16
FIRST_CAST_BLOCKS = 8
VMEM_LIMIT_BYTES = 56 * 1024 * 1024
MASKED = -1e30
LOG2_E = 1.4426950408889634
TILE_PAIR = 2
LONG_RUN = 4
F32 = jnp.float32
BF16 = jnp.bfloat16


def _dot(a, b):
    return jnp.dot(a, b, preferred_element_type=F32)


def _rmsnorm(x, g):
    return x * lax.rsqrt(jnp.mean(x * x, axis=-1, keepdims=True) + EPS) * g


def _swiglu_residual(x, norm_g, wg_ref, wu_ref, wd_ref):
    out = []
    _swiglu_residual_groups([x], norm_g, wg_ref, wu_ref, wd_ref, lambda g, h: out.append(h))
    return out[0]


def _swiglu_residual_groups(xs, norm_g, wg_ref, wu_ref, wd_ref, finish):
    bounds, start = [], 0
    for width in FF_CHUNKS:
        bounds.append((start, width))
        start += width
    steps = [(g, ci) for g in range(len(xs)) for ci in range(len(bounds))]
    normed = {}

    def gate_up(g, ci):
        if g not in normed:
            normed[g] = _rmsnorm(xs[g], norm_g).astype(BF16)
        lo, width = bounds[ci]
        return _dot(normed[g], wg_ref[:, lo:lo + width]), _dot(normed[g], wu_ref[:, lo:lo + width])

    acc = None
    ahead = gate_up(*steps[0])
    for t, (g, ci) in enumerate(steps):
        gate, up = ahead
        if t + 1 < len(steps):
            ahead = gate_up(*steps[t + 1])
        lo, width = bounds[ci]
        down = _dot((gate * jax.nn.sigmoid(gate) * up).astype(BF16), wd_ref[lo:lo + width, :])
        acc = down if ci == 0 else acc + down
        if ci == len(bounds) - 1:
            finish(g, xs[g] + 0.5 * acc)


def _ffn1_qkv_kernel(x_ref, g1_ref, wg_ref, wu_ref, wd_ref, gmix_ref, wqkv_ref, *rest):
    k = (len(rest) - 4) // 2
    side_in, (h_ref, qkv_ref, kmean_ref), side_out = rest[:k + 1], rest[k + 1:k + 4], rest[k + 4:]
    for src, dst in zip(side_in[:-2], side_out[:-1]):
        dst[...] = src[...].astype(BF16)
    half = side_in[-1].shape[-1]
    side_out[-1][:, :half] = side_in[-2][...].astype(BF16)
    side_out[-1][:, half:] = side_in[-1][...].astype(BF16)
    rows = x_ref.shape[0] // 2
    groups = [slice(0, rows), slice(rows, 2 * rows)]

    def finish(g, hg):
        h_ref[groups[g], :] = hg

    _swiglu_residual_groups([x_ref[sl, :] for sl in groups], g1_ref[...], wg_ref, wu_ref, wd_ref, finish)
    h = h_ref[...]
    n = _rmsnorm(h, gmix_ref[...]).astype(BF16)
    z = _dot(n, wqkv_ref[...])
    qkv_ref[...] = z.astype(BF16)
    for j in range(TOKEN_TILE // MOBA_BLOCK):
        kb = z[j * MOBA_BLOCK:(j + 1) * MOBA_BLOCK, A_WIDTH:2 * A_WIDTH]
        kmean_ref[j] = jnp.mean(kb, axis=0, keepdims=True)


def _cast_kernel(*refs):
    n = len(refs) // 2
    for src, dst in zip(refs[:n], refs[n:]):
        dst[...] = src[...].astype(BF16)


def _ffn2_out_kernel(h_ref, g2_ref, wg_ref, wu_ref, wd_ref, gfin_ref, o_ref):
    rows = h_ref.shape[0] // FFN2_ROW_GROUPS
    groups = [slice(g * rows, (g + 1) * rows) for g in range(FFN2_ROW_GROUPS)]

    def finish(g, h):
        o_ref[groups[g], :] = _rmsnorm(h, gfin_ref[...])

    _swiglu_residual_groups([h_ref[sl, :] for sl in groups], g2_ref[...], wg_ref, wu_ref, wd_ref, finish)


def _moba_kernel(q_ref, k_ref, v_ref, kmean_ref, o_ref, vta_ref, qta_ref, s_ref, smax_ref, acc_ref):
    j = pl.program_id(1)
    seq = k_ref.shape[1]
    n_blocks = seq // MOBA_BLOCK
    scale = A_HEAD_DIM ** -0.5
    first_q = TILE_PAIR * j
    n_chains = TILE_PAIR * A_HEADS
    all_chains = list(range(n_chains))
    last_tile_chains = all_chains[-A_HEADS:]

    @pl.when(j == 0)
    def _():
        ones_row = (lax.broadcasted_iota(jnp.int32, (BF16_ROWS, seq), 0) == 0).astype(BF16)
        for h in range(A_HEADS):
            vta_ref[h * V_ROWS + A_HEAD_DIM:(h + 1) * V_ROWS, :] = ones_row
        for c in all_chains:
            qta_ref[c, LANES + n_blocks:, :] = jnp.zeros((LANES - n_blocks, MOBA_BLOCK), BF16)
        for n in range(n_blocks):
            blk = slice(n * MOBA_BLOCK, (n + 1) * MOBA_BLOCK)
            vt = v_ref[0, blk, :].astype(F32).T.astype(BF16)
            for h in range(A_HEADS):
                vta_ref[h * V_ROWS:h * V_ROWS + A_HEAD_DIM, blk] = vt[h * A_HEAD_DIM:(h + 1) * A_HEAD_DIM, :]

    lane = lax.broadcasted_iota(jnp.int32, (MOBA_BLOCK, LANES), 1)
    feat = lax.broadcasted_iota(jnp.int32, (LANES, MOBA_BLOCK), 0)
    blk_row = lax.broadcasted_iota(jnp.int32, (n_blocks, MOBA_BLOCK), 0)
    key_pos = lax.broadcasted_iota(jnp.int32, (MOBA_BLOCK, MOBA_BLOCK), 0)
    qry_pos = lax.broadcasted_iota(jnp.int32, (MOBA_BLOCK, MOBA_BLOCK), 1)

    def score_head(n, slot, c, own=False):
        pair = (c % A_HEADS) // 2
        start = pl.multiple_of(n * MOBA_BLOCK, MOBA_BLOCK)
        k_pair = k_ref[0, pl.ds(start, MOBA_BLOCK), pair * LANES:(pair + 1) * LANES]
        if own:
            s = jnp.where(key_pos <= qry_pos, _dot(k_pair, qta_ref[c, :LANES, :]), MASKED)
        else:
            s = _dot(jnp.concatenate([k_pair, (lane == n).astype(BF16)], axis=1), qta_ref[c])
        s_ref[slot, c] = s
        smax_ref[slot, c] = jnp.max(s, axis=0, keepdims=True)

    for tile in range(TILE_PAIR):
        qi = first_q + tile
        valid = blk_row < qi
        rows = slice(tile * MOBA_BLOCK, (tile + 1) * MOBA_BLOCK)
        for pair in range(HEAD_PAIRS):
            cols = slice(pair * LANES, (pair + 1) * LANES)
            qt = (q_ref[0, rows, cols].astype(F32) * scale).T
            kmean = kmean_ref[0, :, cols].astype(BF16)
            for half in range(2):
                c = tile * A_HEADS + 2 * pair + half
                in_head = (feat >= half * A_HEAD_DIM) & (feat < (half + 1) * A_HEAD_DIM)
                qth = jnp.where(in_head, qt, 0.0).astype(BF16)
                qta_ref[c, :LANES, :] = jnp.where(in_head, qt * LOG2_E, 0.0).astype(BF16)
                score_head(qi, 0, c, own=True)
                gate = jnp.where(valid, _dot(kmean, qth), -jnp.inf)
                selected = jnp.zeros(gate.shape, jnp.bool_)
                for _ in range(MOBA_TOPK):
                    best = jnp.max(gate, axis=0, keepdims=True)
                    first = jnp.min(jnp.where(gate == best, blk_row, n_blocks), axis=0, keepdims=True)
                    pick = blk_row == first
                    selected = selected | pick
                    gate = jnp.where(pick, -jnp.inf, gate)
                selected = selected & valid
                qta_ref[c, LANES:LANES + n_blocks, :] = jnp.where(selected, 0.0, MASKED).astype(BF16)

    def update_head(n, slot, c, m_prev):
        h = c % A_HEADS
        start = pl.multiple_of(n * MOBA_BLOCK, MOBA_BLOCK)
        m_new = smax_ref[slot, c]
        if m_prev is not None:
            m_new = jnp.maximum(m_prev, m_new)
        p = jnp.exp2(s_ref[slot, c] - m_new).astype(BF16)
        pv = _dot(vta_ref[h * V_ROWS:(h + 1) * V_ROWS, pl.ds(start, MOBA_BLOCK)], p)
        if m_prev is None:
            acc_ref[c] = pv
        else:
            acc_ref[c] = jnp.exp2(m_prev - m_new) * acc_ref[c] + pv
        return m_new

    def advance(n, slot, chains, m_run, n_next=None, chains_next=()):
        m_out = list(m_run) if m_run is not None else [None] * n_chains
        todo = list(chains_next)
        for c in chains:
            if todo:
                score_head(n_next, 1 - slot, todo.pop(0))
            m_out[c] = update_head(n(c) if callable(n) else n, slot, c, None if m_run is None else m_run[c])
        for c in todo:
            score_head(n_next, 1 - slot, c)
        return tuple(m_out)

    m_run = advance(lambda c: first_q + c // A_HEADS, 0, all_chains, None, 0, all_chains)

    def run_of_blocks(n0, count, m_run):
        for i in range(count):
            m_run = advance(n0 + i, (i + 1) % 2, all_chains, m_run, n0 + i + 1, all_chains)
        return m_run

    ahead = jnp.maximum(first_q - 1, 0)
    long_trips = lax.shift_right_logical(ahead, LONG_RUN.bit_length() - 1)
    pairs_start = long_trips * LONG_RUN
    pair_trips = lax.shift_right_logical(ahead - pairs_start, 1)
    single_start = pairs_start + 2 * pair_trips
    m_run = lax.fori_loop(0, long_trips, lambda i, m: run_of_blocks(i * LONG_RUN, LONG_RUN, m), m_run)
    m_run = lax.fori_loop(0, pair_trips, lambda i, m: run_of_blocks(pairs_start + 2 * i, 2, m), m_run)
    m_run = lax.fori_loop(0, ahead - single_start, lambda i, m: run_of_blocks(single_start, 1, m), m_run)
    m_run = lax.fori_loop(0, jnp.where(j > 0, 1, 0),
                          lambda i, m: advance(ahead, 0, all_chains, m, first_q, last_tile_chains), m_run)
    advance(first_q, 1, last_tile_chains, m_run)

    for tile in range(TILE_PAIR):
        rows = slice(tile * MOBA_BLOCK, (tile + 1) * MOBA_BLOCK)
        for pair in range(HEAD_PAIRS):
            outs = []
            for half in range(2):
                acc = acc_ref[tile * A_HEADS + 2 * pair + half]
                outs.append(acc[:A_HEAD_DIM] / acc[A_HEAD_DIM:A_HEAD_DIM + 1])
            o_ref[0, rows, pair * LANES:(pair + 1) * LANES] = jnp.concatenate(outs, axis=0).T.astype(o_ref.dtype)


def _mix_kernel(h_ref, attn_ref, gmix_ref, wuv_ref, lng_ref, lnb_ref, ws_ref, bs_ref,
                wba_ref, wbg_ref, wgate_ref, bgate_ref, wout_ref, o_ref):
    rows = h_ref.shape[0] // MIX_ROW_GROUPS
    n_chunks = rows // G_CHUNK
    row = lax.broadcasted_iota(jnp.int32, (G_CHUNK, G_CHUNK), 0)
    col = lax.broadcasted_iota(jnp.int32, (G_CHUNK, G_CHUNK), 1)
    lane = lax.broadcasted_iota(jnp.int32, (G_CHUNK, n_chunks * LANES), 1)
    low_half = (lane % LANES) < G_GROUP_DIM

    def input_matmuls(r):
        sl = slice(r * rows, (r + 1) * rows)
        h = h_ref[sl, :]
        n = _rmsnorm(h, gmix_ref[...]).astype(BF16)
        return sl, h, _dot(n, wuv_ref[...]), _dot(n, wgate_ref[...])

    staged = [input_matmuls(0)]
    for r in range(MIX_ROW_GROUPS):
        if r + 1 < MIX_ROW_GROUPS:
            staged.append(input_matmuls(r + 1))
        sl, h, z_uv, gate_logits = staged[r]
        zg = jax.nn.gelu(z_uv, approximate=True)
        u = zg[:, :G_WIDTH]
        v = zg[:, G_WIDTH:]
        mu = jnp.mean(v, axis=-1, keepdims=True)
        var = jnp.mean(jnp.square(v - mu), axis=-1, keepdims=True)
        v = ((v - mu) * lax.rsqrt(var + EPS) * lng_ref[...] + lnb_ref[...]).astype(BF16)
        mixed_cols = []
        for p in range(GROUP_PAIRS):
            vp = jnp.concatenate(
                [v[c * G_CHUNK:(c + 1) * G_CHUNK, p * LANES:(p + 1) * LANES] for c in range(n_chunks)], axis=1)
            zeros = jnp.zeros_like(vp)
            rhs = jnp.concatenate([jnp.where(low_half, vp, zeros), jnp.where(low_half, zeros, vp)], axis=0)
            w_lo = jnp.where(col <= row, ws_ref[2 * p], 0.0).astype(BF16)
            w_hi = jnp.where(col <= row, ws_ref[2 * p + 1], 0.0).astype(BF16)
            mp = _dot(jnp.concatenate([w_lo, w_hi], axis=1), rhs)
            mixed_cols.append(jnp.concatenate(
                [mp[:, c * LANES:(c + 1) * LANES] for c in range(n_chunks)], axis=0))
        mixed = jnp.concatenate(mixed_cols, axis=1)
        bias = jnp.concatenate([bs_ref[...]] * n_chunks, axis=0)
        gm = (u * (mixed + bias)).astype(BF16)

        y_attn = _dot(attn_ref[sl, :], wba_ref[...])
        y_gmlp = _dot(gm, wbg_ref[...])
        gates = jax.nn.sigmoid(gate_logits + bgate_ref[...])
        merged = gates[:, :D_MODEL] * y_attn + gates[:, D_MODEL:] * y_gmlp
        o_ref[sl, :] = h + _dot(merged.astype(BF16), wout_ref[...])


def _resident(shape):
    zeros = (0,) * len(shape)
    return pl.BlockSpec(shape, lambda *_: zeros, pipeline_mode=pl.Buffered(1))


def _row_tiles(width, rows=TOKEN_TILE):
    return pl.BlockSpec((rows, width), lambda i: (i, 0))


def _dense_params():
    return pltpu.CompilerParams(dimension_semantics=("parallel",), vmem_limit_bytes=VMEM_LIMIT_BYTES)


def kernel(x, ffn1_norm, ffn1_w_gate, ffn1_w_up, ffn1_w_down, mix_norm, w_in, gmlp_ln_g, gmlp_ln_b, gmlp_w_s, gmlp_b_s, w_branch_attn, w_branch_gmlp, w_gate, b_gate, w_out, ffn2_norm, ffn2_w_gate, ffn2_w_up, ffn2_w_down, final_norm):
    B, S, D = x.shape
    T = B * S
    n_blocks = S // MOBA_BLOCK
    assert D == D_MODEL and S % MOBA_BLOCK == 0 and T % TOKEN_TILE == 0
    assert TOKEN_TILE % MOBA_BLOCK == 0 and TOKEN_TILE % G_CHUNK == 0
    assert n_blocks <= LANES, "one one-hot lane per key block"
    assert ffn1_norm.shape[0] == 1, "single-layer block"
    vec = lambda w: w.reshape(1, -1).astype(F32)
    n_tiles = T // TOKEN_TILE
    x2 = x.reshape(T, D)

    first = [(ffn1_w_gate, D_FF), (ffn1_w_up, D_FF), (ffn1_w_down, D), (w_in, 3 * A_WIDTH)]
    assert all(w.shape[1] % (FIRST_CAST_BLOCKS * BF16_ROWS) == 0 for w, _ in first)
    ffn1_wg, ffn1_wu, ffn1_wd, w_qkv = pl.pallas_call(
        _cast_kernel,
        grid=(FIRST_CAST_BLOCKS,),
        in_specs=[pl.BlockSpec((None, w.shape[1] // FIRST_CAST_BLOCKS, cols), lambda i: (0, i, 0))
                  for w, cols in first],
        out_specs=[pl.BlockSpec((w.shape[1] // FIRST_CAST_BLOCKS, cols), lambda i: (i, 0)) for w, cols in first],
        out_shape=[jax.ShapeDtypeStruct((w.shape[1], cols), BF16) for w, cols in first],
        compiler_params=_dense_params(),
        name="cast_first",
    )(*[w for w, _ in first])

    later = [ffn2_w_gate, ffn2_w_up, ffn2_w_down, w_gate, w_out, w_branch_attn, w_branch_gmlp]
    steps_per_block, rest = divmod(n_tiles, SIDE_CAST_BLOCKS)
    assert rest == 0 and all(w.shape[1] % (SIDE_CAST_BLOCKS * BF16_ROWS) == 0 for w in later + [w_in])
    row_block = lambda i: i // steps_per_block
    side_in_specs = [pl.BlockSpec((None, w.shape[1] // SIDE_CAST_BLOCKS, w.shape[2]), lambda i: (0, row_block(i), 0))
                     for w in later]
    side_out_specs = [pl.BlockSpec((w.shape[1] // SIDE_CAST_BLOCKS, w.shape[2]), lambda i: (row_block(i), 0))
                      for w in later]
    side_out_shapes = [jax.ShapeDtypeStruct(w.shape[1:], BF16) for w in later]
    assert (3 * A_WIDTH) % G_WIDTH == 0
    first_g = 3 * A_WIDTH // G_WIDTH
    side_in_specs += [pl.BlockSpec((None, D // SIDE_CAST_BLOCKS, G_WIDTH), lambda i, c=c: (0, row_block(i), first_g + c))
                      for c in range(2)]
    side_out_specs.append(pl.BlockSpec((D // SIDE_CAST_BLOCKS, 2 * G_WIDTH), lambda i: (row_block(i), 0)))
    side_out_shapes.append(jax.ShapeDtypeStruct((D, 2 * G_WIDTH), BF16))
    h1, qkv, kmean, *later_bf16 = pl.pallas_call(
        _ffn1_qkv_kernel,
        grid=(n_tiles,),
        in_specs=[_row_tiles(D), _resident((1, D)), _resident((D, D_FF)), _resident((D, D_FF)),
                  _resident((D_FF, D)), _resident((1, D)), _resident((D, 3 * A_WIDTH))] + side_in_specs,
        out_specs=[_row_tiles(D), _row_tiles(3 * A_WIDTH),
                   pl.BlockSpec((TOKEN_TILE // MOBA_BLOCK, 1, A_WIDTH), lambda i: (i, 0, 0))] + side_out_specs,
        out_shape=[jax.ShapeDtypeStruct((T, D), F32), jax.ShapeDtypeStruct((T, 3 * A_WIDTH), BF16),
                   jax.ShapeDtypeStruct((T // MOBA_BLOCK, 1, A_WIDTH), F32)] + side_out_shapes,
        compiler_params=pltpu.CompilerParams(dimension_semantics=("arbitrary",),
                                             vmem_limit_bytes=VMEM_LIMIT_BYTES),
        name="ffn1_qkv",
    )(x2, vec(ffn1_norm), ffn1_wg, ffn1_wu, ffn1_wd, vec(mix_norm), w_qkv, *later, w_in, w_in)
    ffn2_wg, ffn2_wu, ffn2_wd, w_gate_bf, w_out_bf, w_ba_bf, w_bg_bf, w_uv_bf = later_bf16

    qkv3 = qkv.reshape(B, S, 3 * A_WIDTH)
    q_rows = TILE_PAIR * MOBA_BLOCK
    n_chains = TILE_PAIR * A_HEADS
    assert TILE_PAIR == 2 and n_blocks % TILE_PAIR == 0
    attn = pl.pallas_call(
        _moba_kernel,
        grid=(B, n_blocks // TILE_PAIR),
        in_specs=[pl.BlockSpec((1, q_rows, A_WIDTH), lambda b, i: (b, i, 0)),
                  pl.BlockSpec((1, S, A_WIDTH), lambda b, i: (b, 0, 1)),
                  pl.BlockSpec((1, S, A_WIDTH), lambda b, i: (b, 0, 2)),
                  pl.BlockSpec((1, n_blocks, A_WIDTH), lambda b, i: (b, 0, 0))],
        out_specs=pl.BlockSpec((1, q_rows, A_WIDTH), lambda b, i: (b, i, 0)),
        out_shape=jax.ShapeDtypeStruct((B, S, A_WIDTH), BF16),
        scratch_shapes=[pltpu.VMEM((A_HEADS * V_ROWS, S), BF16),
                        pltpu.VMEM((n_chains, 2 * LANES, MOBA_BLOCK), BF16),
                        pltpu.VMEM((2, n_chains, MOBA_BLOCK, MOBA_BLOCK), F32),
                        pltpu.VMEM((2, n_chains, 1, MOBA_BLOCK), F32),
                        pltpu.VMEM((n_chains, V_ROWS, MOBA_BLOCK), F32)],
        compiler_params=pltpu.CompilerParams(dimension_semantics=("parallel", "arbitrary"),
                                             vmem_limit_bytes=VMEM_LIMIT_BYTES),
        name="moba_attn",
    )(qkv3, qkv3, qkv3, kmean.reshape(B, n_blocks, A_WIDTH))

    bs_lanes = jnp.repeat(gmlp_b_s[0].T, G_GROUP_DIM, axis=1).astype(F32)
    h2 = pl.pallas_call(
        _mix_kernel,
        grid=(T // MIX_TILE,),
        in_specs=[_row_tiles(D, MIX_TILE), _row_tiles(A_WIDTH, MIX_TILE), _resident((1, D)), _resident((D, 2 * G_WIDTH)),
                  _resident((1, G_WIDTH)), _resident((1, G_WIDTH)),
                  _resident((G_GROUPS, G_CHUNK, G_CHUNK)), _resident((G_CHUNK, G_WIDTH)),
                  _resident((A_WIDTH, D)), _resident((G_WIDTH, D)), _resident((D, 2 * D)),
                  _resident((1, 2 * D)), _resident((D, D))],
        out_specs=_row_tiles(D, MIX_TILE),
        out_shape=jax.ShapeDtypeStruct((T, D), F32),
        compiler_params=_dense_params(),
        name="mix",
    )(h1, attn.reshape(T, A_WIDTH), vec(mix_norm), w_uv_bf, vec(gmlp_ln_g),
      vec(gmlp_ln_b), gmlp_w_s[0].astype(F32), bs_lanes, w_ba_bf, w_bg_bf,
      w_gate_bf, vec(b_gate), w_out_bf)

    out = pl.pallas_call(
        _ffn2_out_kernel,
        grid=(T // FFN2_TILE,),
        in_specs=[_row_tiles(D, FFN2_TILE), _resident((1, D)), _resident((D, D_FF)), _resident((D, D_FF)),
                  _resident((D_FF, D)), _resident((1, D))],
        out_specs=_row_tiles(D, FFN2_TILE),
        out_shape=jax.ShapeDtypeStruct((T, D), F32),
        compiler_params=_dense_params(),
        name="ffn2_out",
    )(h2, vec(ffn2_norm), ffn2_wg, ffn2_wu, ffn2_wd, vec(final_norm))
    return out.reshape(B, S, D)
```

```python
import jax
import jax.numpy as jnp
from jax import lax
from jax.experimental import pallas as pl
from jax.experimental.pallas import tpu as pltpu

D_MODEL = 1024
D_FF = 2816
A_HEADS = 8
A_HEAD_DIM = 64
A_WIDTH = A_HEADS * A_HEAD_DIM
MOBA_BLOCK = 256
MOBA_TOPK = 3
G_GROUPS = 8
G_CHUNK = 128
G_WIDTH = 512
G_GROUP_DIM = G_WIDTH // G_GROUPS
EPS = 1e-6

LANES = 128
BF16_ROWS = 16
V_ROWS = A_HEAD_DIM + BF16_ROWS
HEAD_PAIRS = A_WIDTH // LANES
GROUP_PAIRS = G_WIDTH // LANES
MXU_COLS = 256
FF_CHUNKS = (768, 768, 768, 512)
assert sum(FF_CHUNKS) == D_FF and all(c % MXU_COLS == 0 for c in FF_CHUNKS)

TOKEN_TILE = 512
FFN1_ROW_GROUPS = 2
FFN2_TILE = 1024
FFN2_ROW_GROUPS = 4
MIX_TILE = 1024
MIX_ROW_GROUPS = 4
SIDE_CAST_BLOCKS = 16
FIRST_CAST_BLOCKS = 8
VMEM_LIMIT_BYTES = 56 * 1024 * 1024
MASKED = -1e30
LOG2_E = 1.4426950408889634
TILE_PAIR = 2
LONG_RUN = 4
F32 = jnp.float32
BF16 = jnp.bfloat16


def _dot(a, b):
    return jnp.dot(a, b, preferred_element_type=F32)


def _rmsnorm(x, g):
    return x * lax.rsqrt(jnp.mean(x * x, axis=-1, keepdims=True) + EPS) * g


def _swiglu_residual(x, norm_g, wg_ref, wu_ref, wd_ref):
    out = []
    _swiglu_residual_groups([x], norm_g, wg_ref, wu_ref, wd_ref, lambda g, h: out.append(h))
    return out[0]


def _swiglu_residual_groups(xs, norm_g, wg_ref, wu_ref, wd_ref, finish):
    bounds, start = [], 0
    for width in FF_CHUNKS:
        bounds.append((start, width))
        start += width
    steps = [(g, ci) for g in range(len(xs)) for ci in range(len(bounds))]
    normed = {}

    def gate_up(g, ci):
        if g not in normed:
            normed[g] = _rmsnorm(xs[g], norm_g).astype(BF16)
        lo, width = bounds[ci]
        return _dot(normed[g], wg_ref[:, lo:lo + width]), _dot(normed[g], wu_ref[:, lo:lo + width])

    acc = None
    ahead = gate_up(*steps[0])
    for t, (g, ci) in enumerate(steps):
        gate, up = ahead
        if t + 1 < len(steps):
            ahead = gate_up(*steps[t + 1])
        lo, width = bounds[ci]
        down = _dot((gate * jax.nn.sigmoid(gate) * up).astype(BF16), wd_ref[lo:lo + width, :])
        acc = down if ci == 0 else acc + down
        if ci == len(bounds) - 1:
            finish(g, xs[g] + 0.5 * acc)


def _ffn1_qkv_kernel(x_ref, g1_ref, wg_ref, wu_ref, wd_ref, gmix_ref, wqkv_ref, *rest):
    k = (len(rest) - 4) // 2
    side_in, (h_ref, qkv_ref, kmean_ref), side_out = rest[:k + 1], rest[k + 1:k + 4], rest[k + 4:]
    for src, dst in zip(side_in[:-2], side_out[:-1]):
        dst[...] = src[...].astype(BF16)
    half = side_in[-1].shape[-1]
    side_out[-1][:, :half] = side_in[-2][...].astype(BF16)
    side_out[-1][:, half:] = side_in[-1][...].astype(BF16)
    rows = x_ref.shape[0] // FFN1_ROW_GROUPS
    groups = [slice(g * rows, (g + 1) * rows) for g in range(FFN1_ROW_GROUPS)]

    def finish(g, hg):
        h_ref[groups[g], :] = hg

    _swiglu_residual_groups([x_ref[sl, :] for sl in groups], g1_ref[...], wg_ref, wu_ref, wd_ref, finish)
    h = h_ref[...]
    n = _rmsnorm(h, gmix_ref[...]).astype(BF16)
    z = _dot(n, wqkv_ref[...])
    qkv_ref[...] = z.astype(BF16)
    for j in range(TOKEN_TILE // MOBA_BLOCK):
        kb = z[j * MOBA_BLOCK:(j + 1) * MOBA_BLOCK, A_WIDTH:2 * A_WIDTH]
        kmean_ref[j] = jnp.mean(kb, axis=0, keepdims=True)


def _cast_kernel(*refs):
    n = len(refs) // 2
    for src, dst in zip(refs[:n], refs[n:]):
        dst[...] = src[...].astype(BF16)


def _ffn2_out_kernel(h_ref, g2_ref, wg_ref, wu_ref, wd_ref, gfin_ref, o_ref):
    rows = h_ref.shape[0] // FFN2_ROW_GROUPS
    groups = [slice(g * rows, (g + 1) * rows) for g in range(FFN2_ROW_GROUPS)]

    def finish(g, h):
        o_ref[groups[g], :] = _rmsnorm(h, gfin_ref[...])

    _swiglu_residual_groups([h_ref[sl, :] for sl in groups], g2_ref[...], wg_ref, wu_ref, wd_ref, finish)


def _moba_kernel(q_ref, k_ref, v_ref, kmean_ref, o_ref, vta_ref, qta_ref, s_ref, smax_ref, acc_ref):
    j = pl.program_id(1)
    seq = k_ref.shape[1]
    n_blocks = seq // MOBA_BLOCK
    scale = A_HEAD_DIM ** -0.5
    first_q = TILE_PAIR * j
    n_chains = TILE_PAIR * A_HEADS
    all_chains = list(range(n_chains))
    last_tile_chains = all_chains[-A_HEADS:]

    @pl.when(j == 0)
    def _():
        ones_row = (lax.broadcasted_iota(jnp.int32, (BF16_ROWS, seq), 0) == 0).astype(BF16)
        for h in range(A_HEADS):
            vta_ref[h * V_ROWS + A_HEAD_DIM:(h + 1) * V_ROWS, :] = ones_row
        for c in all_chains:
            qta_ref[c, LANES + n_blocks:, :] = jnp.zeros((LANES - n_blocks, MOBA_BLOCK), BF16)
        for n in range(n_blocks):
            blk = slice(n * MOBA_BLOCK, (n + 1) * MOBA_BLOCK)
            vt = v_ref[0, blk, :].astype(F32).T.astype(BF16)
            for h in range(A_HEADS):
                vta_ref[h * V_ROWS:h * V_ROWS + A_HEAD_DIM, blk] = vt[h * A_HEAD_DIM:(h + 1) * A_HEAD_DIM, :]

    lane = lax.broadcasted_iota(jnp.int32, (MOBA_BLOCK, LANES), 1)
    feat = lax.broadcasted_iota(jnp.int32, (LANES, MOBA_BLOCK), 0)
    blk_row = lax.broadcasted_iota(jnp.int32, (n_blocks, MOBA_BLOCK), 0)
    key_pos = lax.broadcasted_iota(jnp.int32, (MOBA_BLOCK, MOBA_BLOCK), 0)
    qry_pos = lax.broadcasted_iota(jnp.int32, (MOBA_BLOCK, MOBA_BLOCK), 1)

    def score_head(n, slot, c, own=False):
        pair = (c % A_HEADS) // 2
        start = pl.multiple_of(n * MOBA_BLOCK, MOBA_BLOCK)
        k_pair = k_ref[0, pl.ds(start, MOBA_BLOCK), pair * LANES:(pair + 1) * LANES]
        if own:
            s = jnp.where(key_pos <= qry_pos, _dot(k_pair, qta_ref[c, :LANES, :]), MASKED)
        else:
            s = _dot(jnp.concatenate([k_pair, (lane == n).astype(BF16)], axis=1), qta_ref[c])
        s_ref[slot, c] = s
        smax_ref[slot, c] = jnp.max(s, axis=0, keepdims=True)

    for tile in range(TILE_PAIR):
        qi = first_q + tile
        valid = blk_row < qi
        rows = slice(tile * MOBA_BLOCK, (tile + 1) * MOBA_BLOCK)
        for pair in range(HEAD_PAIRS):
            cols = slice(pair * LANES, (pair + 1) * LANES)
            qt = (q_ref[0, rows, cols].astype(F32) * scale).T
            kmean = kmean_ref[0, :, cols].astype(BF16)
            for half in range(2):
                c = tile * A_HEADS + 2 * pair + half
                in_head = (feat >= half * A_HEAD_DIM) & (feat < (half + 1) * A_HEAD_DIM)
                qth = jnp.where(in_head, qt, 0.0).astype(BF16)
                qta_ref[c, :LANES, :] = jnp.where(in_head, qt * LOG2_E, 0.0).astype(BF16)
                score_head(qi, 0, c, own=True)
                gate = jnp.where(valid, _dot(kmean, qth), -jnp.inf)
                selected = jnp.zeros(gate.shape, jnp.bool_)
                for _ in range(MOBA_TOPK):
                    best = jnp.max(gate, axis=0, keepdims=True)
                    first = jnp.min(jnp.where(gate == best, blk_row, n_blocks), axis=0, keepdims=True)
                    pick = blk_row == first
                    selected = selected | pick
                    gate = jnp.where(pick, -jnp.inf, gate)
                selected = selected & valid
                qta_ref[c, LANES:LANES + n_blocks, :] = jnp.where(selected, 0.0, MASKED).astype(BF16)

    def update_head(n, slot, c, m_prev):
        h = c % A_HEADS
        start = pl.multiple_of(n * MOBA_BLOCK, MOBA_BLOCK)
        m_new = smax_ref[slot, c]
        if m_prev is not None:
            m_new = jnp.maximum(m_prev, m_new)
        p = jnp.exp2(s_ref[slot, c] - m_new).astype(BF16)
        pv = _dot(vta_ref[h * V_ROWS:(h + 1) * V_ROWS, pl.ds(start, MOBA_BLOCK)], p)
        if m_prev is None:
            acc_ref[c] = pv
        else:
            acc_ref[c] = jnp.exp2(m_prev - m_new) * acc_ref[c] + pv
        return m_new

    def advance(n, slot, chains, m_run, n_next=None, chains_next=()):
        m_out = list(m_run) if m_run is not None else [None] * n_chains
        todo = list(chains_next)
        for c in chains:
            if todo:
                score_head(n_next, 1 - slot, todo.pop(0))
            m_out[c] = update_head(n(c) if callable(n) else n, slot, c, None if m_run is None else m_run[c])
        for c in todo:
            score_head(n_next, 1 - slot, c)
        return tuple(m_out)

    m_run = advance(lambda c: first_q + c // A_HEADS, 0, all_chains, None, 0, all_chains)

    def run_of_blocks(n0, count, m_run):
        for i in range(count):
            m_run = advance(n0 + i, (i + 1) % 2, all_chains, m_run, n0 + i + 1, all_chains)
        return m_run

    ahead = jnp.maximum(first_q - 1, 0)
    long_trips = lax.shift_right_logical(ahead, LONG_RUN.bit_length() - 1)
    pairs_start = long_trips * LONG_RUN
    pair_trips = lax.shift_right_logical(ahead - pairs_start, 1)
    single_start = pairs_start + 2 * pair_trips
    m_run = lax.fori_loop(0, long_trips, lambda i, m: run_of_blocks(i * LONG_RUN, LONG_RUN, m), m_run)
    m_run = lax.fori_loop(0, pair_trips, lambda i, m: run_of_blocks(pairs_start + 2 * i, 2, m), m_run)
    m_run = lax.fori_loop(0, ahead - single_start, lambda i, m: run_of_blocks(single_start, 1, m), m_run)
    m_run = lax.fori_loop(0, jnp.where(j > 0, 1, 0),
                          lambda i, m: advance(ahead, 0, all_chains, m, first_q, last_tile_chains), m_run)
    advance(first_q, 1, last_tile_chains, m_run)

    for tile in range(TILE_PAIR):
        rows = slice(tile * MOBA_BLOCK, (tile + 1) * MOBA_BLOCK)
        for pair in range(HEAD_PAIRS):
            outs = []
            for half in range(2):
                acc = acc_ref[tile * A_HEADS + 2 * pair + half]
                outs.append(acc[:A_HEAD_DIM] / acc[A_HEAD_DIM:A_HEAD_DIM + 1])
            o_ref[0, rows, pair * LANES:(pair + 1) * LANES] = jnp.concatenate(outs, axis=0).T.astype(o_ref.dtype)


def _mix_kernel(h_ref, attn_ref, gmix_ref, wuv_ref, lng_ref, lnb_ref, ws_ref, bs_ref,
                wba_ref, wbg_ref, wgate_ref, bgate_ref, wout_ref, o_ref):
    rows = h_ref.shape[0] // MIX_ROW_GROUPS
    n_chunks = rows // G_CHUNK
    row = lax.broadcasted_iota(jnp.int32, (G_CHUNK, G_CHUNK), 0)
    col = lax.broadcasted_iota(jnp.int32, (G_CHUNK, G_CHUNK), 1)
    lane = lax.broadcasted_iota(jnp.int32, (G_CHUNK, n_chunks * LANES), 1)
    low_half = (lane % LANES) < G_GROUP_DIM

    def input_matmuls(r):
        sl = slice(r * rows, (r + 1) * rows)
        h = h_ref[sl, :]
        n = _rmsnorm(h, gmix_ref[...]).astype(BF16)
        return sl, h, _dot(n, wuv_ref[...]), _dot(n, wgate_ref[...])

    staged = [input_matmuls(0)]
    for r in range(MIX_ROW_GROUPS):
        if r + 1 < MIX_ROW_GROUPS:
            staged.append(input_matmuls(r + 1))
        sl, h, z_uv, gate_logits = staged[r]
        zg = jax.nn.gelu(z_uv, approximate=True)
        u = zg[:, :G_WIDTH]
        v = zg[:, G_WIDTH:]
        mu = jnp.mean(v, axis=-1, keepdims=True)
        var = jnp.mean(jnp.square(v - mu), axis=-1, keepdims=True)
        v = ((v - mu) * lax.rsqrt(var + EPS) * lng_ref[...] + lnb_ref[...]).astype(BF16)
        mixed_cols = []
        for p in range(GROUP_PAIRS):
            vp = jnp.concatenate(
                [v[c * G_CHUNK:(c + 1) * G_CHUNK, p * LANES:(p + 1) * LANES] for c in range(n_chunks)], axis=1)
            zeros = jnp.zeros_like(vp)
            rhs = jnp.concatenate([jnp.where(low_half, vp, zeros), jnp.where(low_half, zeros, vp)], axis=0)
            w_lo = jnp.where(col <= row, ws_ref[2 * p], 0.0).astype(BF16)
            w_hi = jnp.where(col <= row, ws_ref[2 * p + 1], 0.0).astype(BF16)
            mp = _dot(jnp.concatenate([w_lo, w_hi], axis=1), rhs)
            mixed_cols.append(jnp.concatenate(
                [mp[:, c * LANES:(c + 1) * LANES] for c in range(n_chunks)], axis=0))
        mixed = jnp.concatenate(mixed_cols, axis=1)
        bias = jnp.concatenate([bs_ref[...]] * n_chunks, axis=0)
        gm = (u * (mixed + bias)).astype(BF16)

        y_attn = _dot(attn_ref[sl, :], wba_ref[...])
        y_gmlp = _dot(gm, wbg_ref[...])
        gates = jax.nn.sigmoid(gate_logits + bgate_ref[...])
        merged = gates[:, :D_MODEL] * y_attn + gates[:, D_MODEL:] * y_gmlp
        o_ref[sl, :] = h + _dot(merged.astype(BF16), wout_ref[...])


def _resident(shape):
    zeros = (0,) * len(shape)
    return pl.BlockSpec(shape, lambda *_: zeros, pipeline_mode=pl.Buffered(1))


def _row_tiles(width, rows=TOKEN_TILE):
    return pl.BlockSpec((rows, width), lambda i: (i, 0))


def _dense_params():
    return pltpu.CompilerParams(dimension_semantics=("parallel",), vmem_limit_bytes=VMEM_LIMIT_BYTES)


def kernel(x, ffn1_norm, ffn1_w_gate, ffn1_w_up, ffn1_w_down, mix_norm, w_in, gmlp_ln_g, gmlp_ln_b, gmlp_w_s, gmlp_b_s, w_branch_attn, w_branch_gmlp, w_gate, b_gate, w_out, ffn2_norm, ffn2_w_gate, ffn2_w_up, ffn2_w_down, final_norm):
    B, S, D = x.shape
    T = B * S
    n_blocks = S // MOBA_BLOCK
    assert D == D_MODEL and S % MOBA_BLOCK == 0 and T % TOKEN_TILE == 0
    assert TOKEN_TILE % MOBA_BLOCK == 0 and TOKEN_TILE % G_CHUNK == 0
    assert n_blocks <= LANES, "one one-hot lane per key block"
    assert ffn1_norm.shape[0] == 1, "single-layer block"
    vec = lambda w: w.reshape(1, -1).astype(F32)
    n_tiles = T // TOKEN_TILE
    x2 = x.reshape(T, D)

    first = [(ffn1_w_gate, D_FF), (ffn1_w_up, D_FF), (ffn1_w_down, D), (w_in, 3 * A_WIDTH)]
    assert all(w.shape[1] % (FIRST_CAST_BLOCKS * BF16_ROWS) == 0 for w, _ in first)
    ffn1_wg, ffn1_wu, ffn1_wd, w_qkv = pl.pallas_call(
        _cast_kernel,
        grid=(FIRST_CAST_BLOCKS,),
        in_specs=[pl.BlockSpec((None, w.shape[1] // FIRST_CAST_BLOCKS, cols), lambda i: (0, i, 0))
                  for w, cols in first],
        out_specs=[pl.BlockSpec((w.shape[1] // FIRST_CAST_BLOCKS, cols), lambda i: (i, 0)) for w, cols in first],
        out_shape=[jax.ShapeDtypeStruct((w.shape[1], cols), BF16) for w, cols in first],
        compiler_params=_dense_params(),
        name="cast_first",
    )(*[w for w, _ in first])

    later = [ffn2_w_gate, ffn2_w_up, ffn2_w_down, w_gate, w_out, w_branch_attn, w_branch_gmlp]
    steps_per_block, rest = divmod(n_tiles, SIDE_CAST_BLOCKS)
    assert rest == 0 and all(w.shape[1] % (SIDE_CAST_BLOCKS * BF16_ROWS) == 0 for w in later + [w_in])
    row_block = lambda i: i // steps_per_block
    side_in_specs = [pl.BlockSpec((None, w.shape[1] // SIDE_CAST_BLOCKS, w.shape[2]), lambda i: (0, row_block(i), 0))
                     for w in later]
    side_out_specs = [pl.BlockSpec((w.shape[1] // SIDE_CAST_BLOCKS, w.shape[2]), lambda i: (row_block(i), 0))
                      for w in later]
    side_out_shapes = [jax.ShapeDtypeStruct(w.shape[1:], BF16) for w in later]
    assert (3 * A_WIDTH) % G_WIDTH == 0
    first_g = 3 * A_WIDTH // G_WIDTH
    side_in_specs += [pl.BlockSpec((None, D // SIDE_CAST_BLOCKS, G_WIDTH), lambda i, c=c: (0, row_block(i), first_g + c))
                      for c in range(2)]
    side_out_specs.append(pl.BlockSpec((D // SIDE_CAST_BLOCKS, 2 * G_WIDTH), lambda i: (row_block(i), 0)))
    side_out_shapes.append(jax.ShapeDtypeStruct((D, 2 * G_WIDTH), BF16))
    h1, qkv, kmean, *later_bf16 = pl.pallas_call(
        _ffn1_qkv_kernel,
        grid=(n_tiles,),
        in_specs=[_row_tiles(D), _resident((1, D)), _resident((D, D_FF)), _resident((D, D_FF)),
                  _resident((D_FF, D)), _resident((1, D)), _resident((D, 3 * A_WIDTH))] + side_in_specs,
        out_specs=[_row_tiles(D), _row_tiles(3 * A_WIDTH),
                   pl.BlockSpec((TOKEN_TILE // MOBA_BLOCK, 1, A_WIDTH), lambda i: (i, 0, 0))] + side_out_specs,
        out_shape=[jax.ShapeDtypeStruct((T, D), F32), jax.ShapeDtypeStruct((T, 3 * A_WIDTH), BF16),
                   jax.ShapeDtypeStruct((T // MOBA_BLOCK, 1, A_WIDTH), F32)] + side_out_shapes,
        compiler_params=pltpu.CompilerParams(dimension_semantics=("arbitrary",),
                                             vmem_limit_bytes=VMEM_LIMIT_BYTES),
        name="ffn1_qkv",
    )(x2, vec(ffn1_norm), ffn1_wg, ffn1_wu, ffn1_wd, vec(mix_norm), w_qkv, *later, w_in, w_in)
    ffn2_wg, ffn2_wu, ffn2_wd, w_gate_bf, w_out_bf, w_ba_bf, w_bg_bf, w_uv_bf = later_bf16

    qkv3 = qkv.reshape(B, S, 3 * A_WIDTH)
    q_rows = TILE_PAIR * MOBA_BLOCK
    n_chains = TILE_PAIR * A_HEADS
    assert TILE_PAIR == 2 and n_blocks % TILE_PAIR == 0
    attn = pl.pallas_call(
        _moba_kernel,
        grid=(B, n_blocks // TILE_PAIR),
        in_specs=[pl.BlockSpec((1, q_rows, A_WIDTH), lambda b, i: (b, i, 0)),
                  pl.BlockSpec((1, S, A_WIDTH), lambda b, i: (b, 0, 1)),
                  pl.BlockSpec((1, S, A_WIDTH), lambda b, i: (b, 0, 2)),
                  pl.BlockSpec((1, n_blocks, A_WIDTH), lambda b, i: (b, 0, 0))],
        out_specs=pl.BlockSpec((1, q_rows, A_WIDTH), lambda b, i: (b, i, 0)),
        out_shape=jax.ShapeDtypeStruct((B, S, A_WIDTH), BF16),
        scratch_shapes=[pltpu.VMEM((A_HEADS * V_ROWS, S), BF16),
                        pltpu.VMEM((n_chains, 2 * LANES, MOBA_BLOCK), BF16),
                        pltpu.VMEM((2, n_chains, MOBA_BLOCK, MOBA_BLOCK), F32),
                        pltpu.VMEM((2, n_chains, 1, MOBA_BLOCK), F32),
                        pltpu.VMEM((n_chains, V_ROWS, MOBA_BLOCK), F32)],
        compiler_params=pltpu.CompilerParams(dimension_semantics=("parallel", "arbitrary"),
                                             vmem_limit_bytes=VMEM_LIMIT_BYTES),
        name="moba_attn",
    )(qkv3, qkv3, qkv3, kmean.reshape(B, n_blocks, A_WIDTH))

    bs_lanes = jnp.repeat(gmlp_b_s[0].T, G_GROUP_DIM, axis=1).astype(F32)
    h2 = pl.pallas_call(
        _mix_kernel,
        grid=(T // MIX_TILE,),
        in_specs=[_row_tiles(D, MIX_TILE), _row_tiles(A_WIDTH, MIX_TILE), _resident((1, D)), _resident((D, 2 * G_WIDTH)),
                  _resident((1, G_WIDTH)), _resident((1, G_WIDTH)),
                  _resident((G_GROUPS, G_CHUNK, G_CHUNK)), _resident((G_CHUNK, G_WIDTH)),
                  _resident((A_WIDTH, D)), _resident((G_WIDTH, D)), _resident((D, 2 * D)),
                  _resident((1, 2 * D)), _resident((D, D))],
        out_specs=_row_tiles(D, MIX_TILE),
        out_shape=jax.ShapeDtypeStruct((T, D), F32),
        compiler_params=_dense_params(),
        name="mix",
    )(h1, attn.reshape(T, A_WIDTH), vec(mix_norm), w_uv_bf, vec(gmlp_ln_g),
      vec(gmlp_ln_b), gmlp_w_s[0].astype(F32), bs_lanes, w_ba_bf, w_bg_bf,
      w_gate_bf, vec(b_gate), w_out_bf)

    out = pl.pallas_call(
        _ffn2_out_kernel,
        grid=(T // FFN2_TILE,),
        in_specs=[_row_tiles(D, FFN2_TILE), _resident((1, D)), _resident((D, D_FF)), _resident((D, D_FF)),
                  _resident((D_FF, D)), _resident((1, D))],
        out_specs=_row_tiles(D, FFN2_TILE),
        out_shape=jax.ShapeDtypeStruct((T, D), F32),
        compiler_params=_dense_params(),
        name="ffn2_out",
    )(h2, vec(ffn2_norm), ffn2_wg, ffn2_wu, ffn2_wd, vec(final_norm))
    return out.reshape(B, S, D)
```

```python
import jax
import jax.numpy as jnp
from jax import lax
from jax.experimental import pallas as pl
from jax.experimental.pallas import tpu as pltpu

D_MODEL = 1024
D_FF = 2816
A_HEADS = 8
A_HEAD_DIM = 64
A_WIDTH = A_HEADS * A_HEAD_DIM
MOBA_BLOCK = 256
MOBA_TOPK = 3
G_GROUPS = 8
G_CHUNK = 128
G_WIDTH = 512
G_GROUP_DIM = G_WIDTH // G_GROUPS
EPS = 1e-6

LANES = 128
BF16_ROWS = 16
V_ROWS = A_HEAD_DIM + BF16_ROWS
HEAD_PAIRS = A_WIDTH // LANES
GROUP_PAIRS = G_WIDTH // LANES
MXU_COLS = 256
FF_CHUNKS = (768, 768, 768, 512)
assert sum(FF_CHUNKS) == D_FF and all(c % MXU_COLS == 0 for c in FF_CHUNKS)

TOKEN_TILE = 512
FFN1_ROW_GROUPS = 2
FFN2_TILE = 1024
FFN2_ROW_GROUPS = 4
MIX_TILE = 1024
MIX_ROW_GROUPS = 4
SIDE_CAST_BLOCKS = 16
FIRST_CAST_BLOCKS = 8
VMEM_LIMIT_BYTES = 56 * 1024 * 1024
MASKED = -1e30
LOG2_E = 1.4426950408889634
TILE_PAIR = 2
LONG_RUN = 4
F32 = jnp.float32
BF16 = jnp.bfloat16


def _dot(a, b):
    return jnp.dot(a, b, preferred_element_type=F32)


def _rmsnorm(x, g):
    return x * lax.rsqrt(jnp.mean(x * x, axis=-1, keepdims=True) + EPS) * g


def _swiglu_residual(x, norm_g, wg_ref, wu_ref, wd_ref):
    out = []
    _swiglu_residual_groups([x], norm_g, wg_ref, wu_ref, wd_ref, lambda g, h: out.append(h))
    return out[0]


def _swiglu_residual_groups(xs, norm_g, wg_ref, wu_ref, wd_ref, finish):
    bounds, start = [], 0
    for width in FF_CHUNKS:
        bounds.append((start, width))
        start += width
    steps = [(g, ci) for g in range(len(xs)) for ci in range(len(bounds))]
    normed = {}

    def gate_up(g, ci):
        if g not in normed:
            normed[g] = _rmsnorm(xs[g], norm_g).astype(BF16)
        lo, width = bounds[ci]
        return _dot(normed[g], wg_ref[:, lo:lo + width]), _dot(normed[g], wu_ref[:, lo:lo + width])

    acc = None
    ahead = gate_up(*steps[0])
    for t, (g, ci) in enumerate(steps):
        gate, up = ahead
        if t + 1 < len(steps):
            ahead = gate_up(*steps[t + 1])
        lo, width = bounds[ci]
        down = _dot((gate * jax.nn.sigmoid(gate) * up).astype(BF16), wd_ref[lo:lo + width, :])
        acc = down if ci == 0 else acc + down
        if ci == len(bounds) - 1:
            finish(g, xs[g] + 0.5 * acc)


def _ffn1_qkv_kernel(x_ref, g1_ref, wg_ref, wu_ref, wd_ref, gmix_ref, wqkv_ref, *rest):
    k = (len(rest) - 4) // 2
    side_in, (h_ref, qkv_ref, kmean_ref), side_out = rest[:k + 1], rest[k + 1:k + 4], rest[k + 4:]
    for src, dst in zip(side_in[:-2], side_out[:-1]):
        dst[...] = src[...].astype(BF16)
    half = side_in[-1].shape[-1]
    side_out[-1][:, :half] = side_in[-2][...].astype(BF16)
    side_out[-1][:, half:] = side_in[-1][...].astype(BF16)
    rows = x_ref.shape[0] // FFN1_ROW_GROUPS
    groups = [slice(g * rows, (g + 1) * rows) for g in range(FFN1_ROW_GROUPS)]

    def finish(g, hg):
        h_ref[groups[g], :] = hg

    _swiglu_residual_groups([x_ref[sl, :] for sl in groups], g1_ref[...], wg_ref, wu_ref, wd_ref, finish)
    h = h_ref[...]
    n = _rmsnorm(h, gmix_ref[...]).astype(BF16)
    z = _dot(n, wqkv_ref[...])
    qkv_ref[...] = z.astype(BF16)
    for j in range(TOKEN_TILE // MOBA_BLOCK):
        kb = z[j * MOBA_BLOCK:(j + 1) * MOBA_BLOCK, A_WIDTH:2 * A_WIDTH]
        kmean_ref[j] = jnp.mean(kb, axis=0, keepdims=True)


def _cast_kernel(*refs):
    n = len(refs) // 2
    for src, dst in zip(refs[:n], refs[n:]):
        dst[...] = src[...].astype(BF16)


def _ffn2_out_kernel(h_ref, g2_ref, wg_ref, wu_ref, wd_ref, gfin_ref, o_ref):
    rows = h_ref.shape[0] // FFN2_ROW_GROUPS
    groups = [slice(g * rows, (g + 1) * rows) for g in range(FFN2_ROW_GROUPS)]

    def finish(g, h):
        o_ref[groups[g], :] = _rmsnorm(h, gfin_ref[...])

    _swiglu_residual_groups([h_ref[sl, :] for sl in groups], g2_ref[...], wg_ref, wu_ref, wd_ref, finish)


def _moba_kernel(q_ref, k_ref, v_ref, kmean_ref, o_ref, vta_ref, qta_ref, s_ref, smax_ref, acc_ref):
    j = pl.program_id(1)
    seq = k_ref.shape[1]
    n_blocks = seq // MOBA_BLOCK
    scale = A_HEAD_DIM ** -0.5
    first_q = TILE_PAIR * j
    n_chains = TILE_PAIR * A_HEADS
    all_chains = list(range(n_chains))
    last_tile_chains = all_chains[-A_HEADS:]

    @pl.when(j == 0)
    def _():
        ones_row = (lax.broadcasted_iota(jnp.int32, (BF16_ROWS, seq), 0) == 0).astype(BF16)
        for h in range(A_HEADS):
            vta_ref[h * V_ROWS + A_HEAD_DIM:(h + 1) * V_ROWS, :] = ones_row
        for c in all_chains:
            qta_ref[c, LANES + n_blocks:, :] = jnp.zeros((LANES - n_blocks, MOBA_BLOCK), BF16)
        for n in range(n_blocks):
            blk = slice(n * MOBA_BLOCK, (n + 1) * MOBA_BLOCK)
            vt = v_ref[0, blk, :].astype(F32).T.astype(BF16)
            for h in range(A_HEADS):
                vta_ref[h * V_ROWS:h * V_ROWS + A_HEAD_DIM, blk] = vt[h * A_HEAD_DIM:(h + 1) * A_HEAD_DIM, :]

    lane = lax.broadcasted_iota(jnp.int32, (MOBA_BLOCK, LANES), 1)
    feat = lax.broadcasted_iota(jnp.int32, (LANES, MOBA_BLOCK), 0)
    blk_row = lax.broadcasted_iota(jnp.int32, (n_blocks, MOBA_BLOCK), 0)
    key_pos = lax.broadcasted_iota(jnp.int32, (MOBA_BLOCK, MOBA_BLOCK), 0)
    qry_pos = lax.broadcasted_iota(jnp.int32, (MOBA_BLOCK, MOBA_BLOCK), 1)

    def score_head(n, slot, c, own=False):
        pair = (c % A_HEADS) // 2
        start = pl.multiple_of(n * MOBA_BLOCK, MOBA_BLOCK)
        k_pair = k_ref[0, pl.ds(start, MOBA_BLOCK), pair * LANES:(pair + 1) * LANES]
        if own:
            s = jnp.where(key_pos <= qry_pos, _dot(k_pair, qta_ref[c, :LANES, :]), MASKED)
        else:
            s = _dot(jnp.concatenate([k_pair, (lane == n).astype(BF16)], axis=1), qta_ref[c])
        s_ref[slot, c] = s
        smax_ref[slot, c] = jnp.max(s, axis=0, keepdims=True)

    for tile in range(TILE_PAIR):
        qi = first_q + tile
        valid = blk_row < qi
        rows = slice(tile * MOBA_BLOCK, (tile + 1) * MOBA_BLOCK)
        for pair in range(HEAD_PAIRS):
            cols = slice(pair * LANES, (pair + 1) * LANES)
            qt = (q_ref[0, rows, cols].astype(F32) * scale).T
            kmean = kmean_ref[0, :, cols].astype(BF16)
            for half in range(2):
                c = tile * A_HEADS + 2 * pair + half
                in_head = (feat >= half * A_HEAD_DIM) & (feat < (half + 1) * A_HEAD_DIM)
                qth = jnp.where(in_head, qt, 0.0).astype(BF16)
                qta_ref[c, :LANES, :] = jnp.where(in_head, qt * LOG2_E, 0.0).astype(BF16)
                score_head(qi, 0, c, own=True)
                gate = jnp.where(valid, _dot(kmean, qth), -jnp.inf)
                selected = jnp.zeros(gate.shape, jnp.bool_)
                for _ in range(MOBA_TOPK):
                    best = jnp.max(gate, axis=0, keepdims=True)
                    first = jnp.min(jnp.where(gate == best, blk_row, n_blocks), axis=0, keepdims=True)
                    pick = blk_row == first
                    selected = selected | pick
                    gate = jnp.where(pick, -jnp.inf, gate)
                selected = selected & valid
                qta_ref[c, LANES:LANES + n_blocks, :] = jnp.where(selected, 0.0, MASKED).astype(BF16)

    def update_head(n, slot, c, m_prev):
        h = c % A_HEADS
        start = pl.multiple_of(n * MOBA_BLOCK, MOBA_BLOCK)
        m_new = smax_ref[slot, c]
        if m_prev is not None:
            m_new = jnp.maximum(m_prev, m_new)
        p = jnp.exp2(s_ref[slot, c] - m_new).astype(BF16)
        pv = _dot(vta_ref[h * V_ROWS:(h + 1) * V_ROWS, pl.ds(start, MOBA_BLOCK)], p)
        if m_prev is None:
            acc_ref[c] = pv
        else:
            acc_ref[c] = jnp.exp2(m_prev - m_new) * acc_ref[c] + pv
        return m_new

    def advance(n, slot, chains, m_run, n_next=None, chains_next=()):
        m_out = list(m_run) if m_run is not None else [None] * n_chains
        todo = list(chains_next)
        for c in chains:
            if todo:
                score_head(n_next, 1 - slot, todo.pop(0))
            m_out[c] = update_head(n(c) if callable(n) else n, slot, c, None if m_run is None else m_run[c])
        for c in todo:
            score_head(n_next, 1 - slot, c)
        return tuple(m_out)

    m_run = advance(lambda c: first_q + c // A_HEADS, 0, all_chains, None, 0, all_chains)

    def run_of_blocks(n0, count, m_run):
        for i in range(count):
            m_run = advance(n0 + i, (i + 1) % 2, all_chains, m_run, n0 + i + 1, all_chains)
        return m_run

    ahead = jnp.maximum(first_q - 1, 0)
    long_trips = lax.shift_right_logical(ahead, LONG_RUN.bit_length() - 1)
    pairs_start = long_trips * LONG_RUN
    pair_trips = lax.shift_right_logical(ahead - pairs_start, 1)
    single_start = pairs_start + 2 * pair_trips
    m_run = lax.fori_loop(0, long_trips, lambda i, m: run_of_blocks(i * LONG_RUN, LONG_RUN, m), m_run)
    m_run = lax.fori_loop(0, pair_trips, lambda i, m: run_of_blocks(pairs_start + 2 * i, 2, m), m_run)
    m_run = lax.fori_loop(0, ahead - single_start, lambda i, m: run_of_blocks(single_start, 1, m), m_run)
    m_run = lax.fori_loop(0, jnp.where(j > 0, 1, 0),
                          lambda i, m: advance(ahead, 0, all_chains, m, first_q, last_tile_chains), m_run)
    advance(first_q, 1, last_tile_chains, m_run)

    for tile in range(TILE_PAIR):
        rows = slice(tile * MOBA_BLOCK, (tile + 1) * MOBA_BLOCK)
        for pair in range(HEAD_PAIRS):
            outs = []
            for half in range(2):
                acc = acc_ref[tile * A_HEADS + 2 * pair + half]
                outs.append(acc[:A_HEAD_DIM] / acc[A_HEAD_DIM:A_HEAD_DIM + 1])
            o_ref[0, rows, pair * LANES:(pair + 1) * LANES] = jnp.concatenate(outs, axis=0).T.astype(o_ref.dtype)


def _mix_kernel(h_ref, attn_ref, gmix_ref, wuv_ref, lng_ref, lnb_ref, ws_ref, bs_ref,
                wba_ref, wbg_ref, wgate_ref, bgate_ref, wout_ref, o_ref):
    rows = h_ref.shape[0] // MIX_ROW_GROUPS
    n_chunks = rows // G_CHUNK
    row = lax.broadcasted_iota(jnp.int32, (G_CHUNK, G_CHUNK), 0)
    col = lax.broadcasted_iota(jnp.int32, (G_CHUNK, G_CHUNK), 1)
    lane = lax.broadcasted_iota(jnp.int32, (G_CHUNK, n_chunks * LANES), 1)
    low_half = (lane % LANES) < G_GROUP_DIM

    def input_matmuls(r):
        sl = slice(r * rows, (r + 1) * rows)
        h = h_ref[sl, :]
        n = _rmsnorm(h, gmix_ref[...]).astype(BF16)
        return sl, h, _dot(n, wuv_ref[...]), _dot(n, wgate_ref[...])

    staged = [input_matmuls(0)]
    for r in range(MIX_ROW_GROUPS):
        if r + 1 < MIX_ROW_GROUPS:
            staged.append(input_matmuls(r + 1))
        sl, h, z_uv, gate_logits = staged[r]
        zg = jax.nn.gelu(z_uv, approximate=True)
        u = zg[:, :G_WIDTH]
        v = zg[:, G_WIDTH:]
        mu = jnp.mean(v, axis=-1, keepdims=True)
        var = jnp.mean(jnp.square(v - mu), axis=-1, keepdims=True)
        v = ((v - mu) * lax.rsqrt(var + EPS) * lng_ref[...] + lnb_ref[...]).astype(BF16)
        mixed_cols = []
        for p in range(GROUP_PAIRS):
            vp = jnp.concatenate(
                [v[c * G_CHUNK:(c + 1) * G_CHUNK, p * LANES:(p + 1) * LANES] for c in range(n_chunks)], axis=1)
            zeros = jnp.zeros_like(vp)
            rhs = jnp.concatenate([jnp.where(low_half, vp, zeros), jnp.where(low_half, zeros, vp)], axis=0)
            w_lo = jnp.where(col <= row, ws_ref[2 * p], 0.0).astype(BF16)
            w_hi = jnp.where(col <= row, ws_ref[2 * p + 1], 0.0).astype(BF16)
            mp = _dot(jnp.concatenate([w_lo, w_hi], axis=1), rhs)
            mixed_cols.append(jnp.concatenate(
                [mp[:, c * LANES:(c + 1) * LANES] for c in range(n_chunks)], axis=0))
        mixed = jnp.concatenate(mixed_cols, axis=1)
        bias = jnp.concatenate([bs_ref[...]] * n_chunks, axis=0)
        gm = (u * (mixed + bias)).astype(BF16)

        y_attn = _dot(attn_ref[sl, :], wba_ref[...])
        y_gmlp = _dot(gm, wbg_ref[...])
        gates = jax.nn.sigmoid(gate_logits + bgate_ref[...])
        merged = gates[:, :D_MODEL] * y_attn + gates[:, D_MODEL:] * y_gmlp
        o_ref[sl, :] = h + _dot(merged.astype(BF16), wout_ref[...])


def _resident(shape):
    zeros = (0,) * len(shape)
    return pl.BlockSpec(shape, lambda *_: zeros, pipeline_mode=pl.Buffered(1))


def _row_tiles(width, rows=TOKEN_TILE):
    return pl.BlockSpec((rows, width), lambda i: (i, 0))


def _dense_params():
    return pltpu.CompilerParams(dimension_semantics=("parallel",), vmem_limit_bytes=VMEM_LIMIT_BYTES)


def kernel(x, ffn1_norm, ffn1_w_gate, ffn1_w_up, ffn1_w_down, mix_norm, w_in, gmlp_ln_g, gmlp_ln_b, gmlp_w_s, gmlp_b_s, w_branch_attn, w_branch_gmlp, w_gate, b_gate, w_out, ffn2_norm, ffn2_w_gate, ffn2_w_up, ffn2_w_down, final_norm):
    B, S, D = x.shape
    T = B * S
    n_blocks = S // MOBA_BLOCK
    assert D == D_MODEL and S % MOBA_BLOCK == 0 and T % TOKEN_TILE == 0
    assert TOKEN_TILE % MOBA_BLOCK == 0 and TOKEN_TILE % G_CHUNK == 0
    assert n_blocks <= LANES, "one one-hot lane per key block"
    assert ffn1_norm.shape[0] == 1, "single-layer block"
    vec = lambda w: w.reshape(1, -1).astype(F32)
    n_tiles = T // TOKEN_TILE
    x2 = x.reshape(T, D)

    ffn1_wg, ffn1_wu, ffn1_wd = (w[0].astype(BF16) for w in (ffn1_w_gate, ffn1_w_up, ffn1_w_down))
    w_qkv = w_in[0, :, :3 * A_WIDTH].astype(BF16)

    later = [ffn2_w_gate, ffn2_w_up, ffn2_w_down, w_gate, w_out, w_branch_attn, w_branch_gmlp]
    steps_per_block, rest = divmod(n_tiles, SIDE_CAST_BLOCKS)
    assert rest == 0 and all(w.shape[1] % (SIDE_CAST_BLOCKS * BF16_ROWS) == 0 for w in later + [w_in])
    row_block = lambda i: i // steps_per_block
    side_in_specs = [pl.BlockSpec((None, w.shape[1] // SIDE_CAST_BLOCKS, w.shape[2]), lambda i: (0, row_block(i), 0))
                     for w in later]
    side_out_specs = [pl.BlockSpec((w.shape[1] // SIDE_CAST_BLOCKS, w.shape[2]), lambda i: (row_block(i), 0))
                      for w in later]
    side_out_shapes = [jax.ShapeDtypeStruct(w.shape[1:], BF16) for w in later]
    assert (3 * A_WIDTH) % G_WIDTH == 0
    first_g = 3 * A_WIDTH // G_WIDTH
    side_in_specs += [pl.BlockSpec((None, D // SIDE_CAST_BLOCKS, G_WIDTH), lambda i, c=c: (0, row_block(i), first_g + c))
                      for c in range(2)]
    side_out_specs.append(pl.BlockSpec((D // SIDE_CAST_BLOCKS, 2 * G_WIDTH), lambda i: (row_block(i), 0)))
    side_out_shapes.append(jax.ShapeDtypeStruct((D, 2 * G_WIDTH), BF16))
    h1, qkv, kmean, *later_bf16 = pl.pallas_call(
        _ffn1_qkv_kernel,
        grid=(n_tiles,),
        in_specs=[_row_tiles(D), _resident((1, D)), _resident((D, D_FF)), _resident((D, D_FF)),
                  _resident((D_FF, D)), _resident((1, D)), _resident((D, 3 * A_WIDTH))] + side_in_specs,
        out_specs=[_row_tiles(D), _row_tiles(3 * A_WIDTH),
                   pl.BlockSpec((TOKEN_TILE // MOBA_BLOCK, 1, A_WIDTH), lambda i: (i, 0, 0))] + side_out_specs,
        out_shape=[jax.ShapeDtypeStruct((T, D), F32), jax.ShapeDtypeStruct((T, 3 * A_WIDTH), BF16),
                   jax.ShapeDtypeStruct((T // MOBA_BLOCK, 1, A_WIDTH), F32)] + side_out_shapes,
        compiler_params=pltpu.CompilerParams(
            dimension_semantics=("arbitrary",), vmem_limit_bytes=VMEM_LIMIT_BYTES,
            allow_input_fusion=[i in (2, 3, 4, 6) for i in range(7 + len(side_in_specs))]),
        name="ffn1_qkv",
    )(x2, vec(ffn1_norm), ffn1_wg, ffn1_wu, ffn1_wd, vec(mix_norm), w_qkv, *later, w_in, w_in)
    ffn2_wg, ffn2_wu, ffn2_wd, w_gate_bf, w_out_bf, w_ba_bf, w_bg_bf, w_uv_bf = later_bf16

    qkv3 = qkv.reshape(B, S, 3 * A_WIDTH)
    q_rows = TILE_PAIR * MOBA_BLOCK
    n_chains = TILE_PAIR * A_HEADS
    assert TILE_PAIR == 2 and n_blocks % TILE_PAIR == 0
    attn = pl.pallas_call(
        _moba_kernel,
        grid=(B, n_blocks // TILE_PAIR),
        in_specs=[pl.BlockSpec((1, q_rows, A_WIDTH), lambda b, i: (b, i, 0)),
                  pl.BlockSpec((1, S, A_WIDTH), lambda b, i: (b, 0, 1)),
                  pl.BlockSpec((1, S, A_WIDTH), lambda b, i: (b, 0, 2)),
                  pl.BlockSpec((1, n_blocks, A_WIDTH), lambda b, i: (b, 0, 0))],
        out_specs=pl.BlockSpec((1, q_rows, A_WIDTH), lambda b, i: (b, i, 0)),
        out_shape=jax.ShapeDtypeStruct((B, S, A_WIDTH), BF16),
        scratch_shapes=[pltpu.VMEM((A_HEADS * V_ROWS, S), BF16),
                        pltpu.VMEM((n_chains, 2 * LANES, MOBA_BLOCK), BF16),
                        pltpu.VMEM((2, n_chains, MOBA_BLOCK, MOBA_BLOCK), F32),
                        pltpu.VMEM((2, n_chains, 1, MOBA_BLOCK), F32),
                        pltpu.VMEM((n_chains, V_ROWS, MOBA_BLOCK), F32)],
        compiler_params=pltpu.CompilerParams(dimension_semantics=("parallel", "arbitrary"),
                                             vmem_limit_bytes=VMEM_LIMIT_BYTES),
        name="moba_attn",
    )(qkv3, qkv3, qkv3, kmean.reshape(B, n_blocks, A_WIDTH))

    bs_lanes = jnp.repeat(gmlp_b_s[0].T, G_GROUP_DIM, axis=1).astype(F32)
    h2 = pl.pallas_call(
        _mix_kernel,
        grid=(T // MIX_TILE,),
        in_specs=[_row_tiles(D, MIX_TILE), _row_tiles(A_WIDTH, MIX_TILE), _resident((1, D)), _resident((D, 2 * G_WIDTH)),
                  _resident((1, G_WIDTH)), _resident((1, G_WIDTH)),
                  _resident((G_GROUPS, G_CHUNK, G_CHUNK)), _resident((G_CHUNK, G_WIDTH)),
                  _resident((A_WIDTH, D)), _resident((G_WIDTH, D)), _resident((D, 2 * D)),
                  _resident((1, 2 * D)), _resident((D, D))],
        out_specs=_row_tiles(D, MIX_TILE),
        out_shape=jax.ShapeDtypeStruct((T, D), F32),
        compiler_params=_dense_params(),
        name="mix",
    )(h1, attn.reshape(T, A_WIDTH), vec(mix_norm), w_uv_bf, vec(gmlp_ln_g),
      vec(gmlp_ln_b), gmlp_w_s[0].astype(F32), bs_lanes, w_ba_bf, w_bg_bf,
      w_gate_bf, vec(b_gate), w_out_bf)

    out = pl.pallas_call(
        _ffn2_out_kernel,
        grid=(T // FFN2_TILE,),
        in_specs=[_row_tiles(D, FFN2_TILE), _resident((1, D)), _resident((D, D_FF)), _resident((D, D_FF)),
                  _resident((D_FF, D)), _resident((1, D))],
        out_specs=_row_tiles(D, FFN2_TILE),
        out_shape=jax.ShapeDtypeStruct((T, D), F32),
        compiler_params=_dense_params(),
        name="ffn2_out",
    )(h2, vec(ffn2_norm), ffn2_wg, ffn2_wu, ffn2_wd, vec(final_norm))
    return out.reshape(B, S, D)
```

```python
import jax
import jax.numpy as jnp
from jax import lax
from jax.experimental import pallas as pl
from jax.experimental.pallas import tpu as pltpu

D_MODEL = 1024
D_FF = 2816
A_HEADS = 8
A_HEAD_DIM = 64
A_WIDTH = A_HEADS * A_HEAD_DIM
MOBA_BLOCK = 256
MOBA_TOPK = 3
G_GROUPS = 8
G_CHUNK = 128
G_WIDTH = 512
G_GROUP_DIM = G_WIDTH // G_GROUPS
EPS = 1e-6

LANES = 128
BF16_ROWS = 16
V_ROWS = A_HEAD_DIM + BF16_ROWS
HEAD_PAIRS = A_WIDTH // LANES
GROUP_PAIRS = G_WIDTH // LANES
MXU_COLS = 256
FF_CHUNKS = (768, 768, 768, 512)
assert sum(FF_CHUNKS) == D_FF and all(c % MXU_COLS == 0 for c in FF_CHUNKS)

TOKEN_TILE = 512
FFN1_ROW_GROUPS = 2
FFN2_TILE = 1024
FFN2_ROW_GROUPS = 4
MIX_TILE = 1024
MIX_ROW_GROUPS = 4
SIDE_CAST_BLOCKS = 16
FIRST_CAST_BLOCKS = 8
VMEM_LIMIT_BYTES = 56 * 1024 * 1024
MASKED = -1e30
LOG2_E = 1.4426950408889634
TILE_PAIR = 2
LONG_RUN = 4
F32 = jnp.float32
BF16 = jnp.bfloat16


def _dot(a, b):
    return jnp.dot(a, b, preferred_element_type=F32)


def _rmsnorm(x, g):
    return x * lax.rsqrt(jnp.mean(x * x, axis=-1, keepdims=True) + EPS) * g


def _swiglu_residual(x, norm_g, wg_ref, wu_ref, wd_ref):
    out = []
    _swiglu_residual_groups([x], norm_g, wg_ref, wu_ref, wd_ref, lambda g, h: out.append(h))
    return out[0]


def _swiglu_residual_groups(xs, norm_g, wg_ref, wu_ref, wd_ref, finish):
    bounds, start = [], 0
    for width in FF_CHUNKS:
        bounds.append((start, width))
        start += width
    steps = [(g, ci) for g in range(len(xs)) for ci in range(len(bounds))]
    normed = {}

    def gate_up(g, ci):
        if g not in normed:
            normed[g] = _rmsnorm(xs[g], norm_g).astype(BF16)
        lo, width = bounds[ci]
        return _dot(normed[g], wg_ref[:, lo:lo + width]), _dot(normed[g], wu_ref[:, lo:lo + width])

    acc = None
    ahead = gate_up(*steps[0])
    for t, (g, ci) in enumerate(steps):
        gate, up = ahead
        if t + 1 < len(steps):
            ahead = gate_up(*steps[t + 1])
        lo, width = bounds[ci]
        down = _dot((gate * jax.nn.sigmoid(gate) * up).astype(BF16), wd_ref[lo:lo + width, :])
        acc = down if ci == 0 else acc + down
        if ci == len(bounds) - 1:
            finish(g, xs[g] + 0.5 * acc)


def _ffn1_qkv_kernel(x_ref, g1_ref, wg_ref, wu_ref, wd_ref, gmix_ref, wqkv_ref, *rest):
    k = (len(rest) - 4) // 2
    side_in, (h_ref, qkv_ref, kmean_ref), side_out = rest[:k + 1], rest[k + 1:k + 4], rest[k + 4:]
    for src, dst in zip(side_in[:-2], side_out[:-1]):
        dst[...] = src[...].astype(BF16)
    half = side_in[-1].shape[-1]
    side_out[-1][:, :half] = side_in[-2][...].astype(BF16)
    side_out[-1][:, half:] = side_in[-1][...].astype(BF16)
    rows = x_ref.shape[0] // FFN1_ROW_GROUPS
    groups = [slice(g * rows, (g + 1) * rows) for g in range(FFN1_ROW_GROUPS)]

    def finish(g, hg):
        h_ref[groups[g], :] = hg

    _swiglu_residual_groups([x_ref[sl, :] for sl in groups], g1_ref[...], wg_ref, wu_ref, wd_ref, finish)
    h = h_ref[...]
    n = _rmsnorm(h, gmix_ref[...]).astype(BF16)
    z = _dot(n, wqkv_ref[...])
    qkv_ref[...] = z.astype(BF16)
    for j in range(TOKEN_TILE // MOBA_BLOCK):
        kb = z[j * MOBA_BLOCK:(j + 1) * MOBA_BLOCK, A_WIDTH:2 * A_WIDTH]
        kmean_ref[j] = jnp.mean(kb, axis=0, keepdims=True)


def _cast_kernel(*refs):
    n = len(refs) // 2
    for src, dst in zip(refs[:n], refs[n:]):
        dst[...] = src[...].astype(BF16)


def _ffn2_out_kernel(h_ref, g2_ref, wg_ref, wu_ref, wd_ref, gfin_ref, o_ref):
    rows = h_ref.shape[0] // FFN2_ROW_GROUPS
    groups = [slice(g * rows, (g + 1) * rows) for g in range(FFN2_ROW_GROUPS)]

    def finish(g, h):
        o_ref[groups[g], :] = _rmsnorm(h, gfin_ref[...])

    _swiglu_residual_groups([h_ref[sl, :] for sl in groups], g2_ref[...], wg_ref, wu_ref, wd_ref, finish)


def _moba_kernel(q_ref, k_ref, v_ref, kmean_ref, o_ref, vta_ref, qta_ref, s_ref, smax_ref, acc_ref):
    j = pl.program_id(1)
    seq = k_ref.shape[1]
    n_blocks = seq // MOBA_BLOCK
    scale = A_HEAD_DIM ** -0.5
    first_q = TILE_PAIR * j
    n_chains = TILE_PAIR * A_HEADS
    all_chains = list(range(n_chains))
    last_tile_chains = all_chains[-A_HEADS:]

    @pl.when(j == 0)
    def _():
        ones_row = (lax.broadcasted_iota(jnp.int32, (BF16_ROWS, seq), 0) == 0).astype(BF16)
        for h in range(A_HEADS):
            vta_ref[h * V_ROWS + A_HEAD_DIM:(h + 1) * V_ROWS, :] = ones_row
        for c in all_chains:
            qta_ref[c, LANES + n_blocks:, :] = jnp.zeros((LANES - n_blocks, MOBA_BLOCK), BF16)
        for n in range(n_blocks):
            blk = slice(n * MOBA_BLOCK, (n + 1) * MOBA_BLOCK)
            vt = v_ref[0, blk, :].astype(F32).T.astype(BF16)
            for h in range(A_HEADS):
                vta_ref[h * V_ROWS:h * V_ROWS + A_HEAD_DIM, blk] = vt[h * A_HEAD_DIM:(h + 1) * A_HEAD_DIM, :]

    lane = lax.broadcasted_iota(jnp.int32, (MOBA_BLOCK, LANES), 1)
    feat = lax.broadcasted_iota(jnp.int32, (LANES, MOBA_BLOCK), 0)
    blk_row = lax.broadcasted_iota(jnp.int32, (n_blocks, MOBA_BLOCK), 0)
    key_pos = lax.broadcasted_iota(jnp.int32, (MOBA_BLOCK, MOBA_BLOCK), 0)
    qry_pos = lax.broadcasted_iota(jnp.int32, (MOBA_BLOCK, MOBA_BLOCK), 1)

    def score_head(n, slot, c, own=False):
        pair = (c % A_HEADS) // 2
        start = pl.multiple_of(n * MOBA_BLOCK, MOBA_BLOCK)
        k_pair = k_ref[0, pl.ds(start, MOBA_BLOCK), pair * LANES:(pair + 1) * LANES]
        if own:
            s = jnp.where(key_pos <= qry_pos, _dot(k_pair, qta_ref[c, :LANES, :]), MASKED)
        else:
            s = _dot(jnp.concatenate([k_pair, (lane == n).astype(BF16)], axis=1), qta_ref[c])
        s_ref[slot, c] = s
        smax_ref[slot, c] = jnp.max(s, axis=0, keepdims=True)

    for tile in range(TILE_PAIR):
        qi = first_q + tile
        valid = blk_row < qi
        rows = slice(tile * MOBA_BLOCK, (tile + 1) * MOBA_BLOCK)
        for pair in range(HEAD_PAIRS):
            cols = slice(pair * LANES, (pair + 1) * LANES)
            qt = (q_ref[0, rows, cols].astype(F32) * scale).T
            kmean = kmean_ref[0, :, cols].astype(BF16)
            for half in range(2):
                c = tile * A_HEADS + 2 * pair + half
                in_head = (feat >= half * A_HEAD_DIM) & (feat < (half + 1) * A_HEAD_DIM)
                qth = jnp.where(in_head, qt, 0.0).astype(BF16)
                qta_ref[c, :LANES, :] = jnp.where(in_head, qt * LOG2_E, 0.0).astype(BF16)
                score_head(qi, 0, c, own=True)
                gate = jnp.where(valid, _dot(kmean, qth), -jnp.inf)
                selected = jnp.zeros(gate.shape, jnp.bool_)
                for _ in range(MOBA_TOPK):
                    best = jnp.max(gate, axis=0, keepdims=True)
                    first = jnp.min(jnp.where(gate == best, blk_row, n_blocks), axis=0, keepdims=True)
                    pick = blk_row == first
                    selected = selected | pick
                    gate = jnp.where(pick, -jnp.inf, gate)
                selected = selected & valid
                qta_ref[c, LANES:LANES + n_blocks, :] = jnp.where(selected, 0.0, MASKED).astype(BF16)

    def update_head(n, slot, c, m_prev):
        h = c % A_HEADS
        start = pl.multiple_of(n * MOBA_BLOCK, MOBA_BLOCK)
        m_new = smax_ref[slot, c]
        if m_prev is not None:
            m_new = jnp.maximum(m_prev, m_new)
        p = jnp.exp2(s_ref[slot, c] - m_new).astype(BF16)
        pv = _dot(vta_ref[h * V_ROWS:(h + 1) * V_ROWS, pl.ds(start, MOBA_BLOCK)], p)
        if m_prev is None:
            acc_ref[c] = pv
        else:
            acc_ref[c] = jnp.exp2(m_prev - m_new) * acc_ref[c] + pv
        return m_new

    def advance(n, slot, chains, m_run, n_next=None, chains_next=()):
        m_out = list(m_run) if m_run is not None else [None] * n_chains
        todo = list(chains_next)
        for c in chains:
            if todo:
                score_head(n_next, 1 - slot, todo.pop(0))
            m_out[c] = update_head(n(c) if callable(n) else n, slot, c, None if m_run is None else m_run[c])
        for c in todo:
            score_head(n_next, 1 - slot, c)
        return tuple(m_out)

    m_run = advance(lambda c: first_q + c // A_HEADS, 0, all_chains, None, 0, all_chains)

    def run_of_blocks(n0, count, m_run):
        for i in range(count):
            m_run = advance(n0 + i, (i + 1) % 2, all_chains, m_run, n0 + i + 1, all_chains)
        return m_run

    ahead = jnp.maximum(first_q - 1, 0)
    long_trips = lax.shift_right_logical(ahead, LONG_RUN.bit_length() - 1)
    pairs_start = long_trips * LONG_RUN
    pair_trips = lax.shift_right_logical(ahead - pairs_start, 1)
    single_start = pairs_start + 2 * pair_trips
    m_run = lax.fori_loop(0, long_trips, lambda i, m: run_of_blocks(i * LONG_RUN, LONG_RUN, m), m_run)
    m_run = lax.fori_loop(0, pair_trips, lambda i, m: run_of_blocks(pairs_start + 2 * i, 2, m), m_run)
    m_run = lax.fori_loop(0, ahead - single_start, lambda i, m: run_of_blocks(single_start, 1, m), m_run)
    m_run = lax.fori_loop(0, jnp.where(j > 0, 1, 0),
                          lambda i, m: advance(ahead, 0, all_chains, m, first_q, last_tile_chains), m_run)
    advance(first_q, 1, last_tile_chains, m_run)

    for tile in range(TILE_PAIR):
        rows = slice(tile * MOBA_BLOCK, (tile + 1) * MOBA_BLOCK)
        for pair in range(HEAD_PAIRS):
            outs = []
            for half in range(2):
                acc = acc_ref[tile * A_HEADS + 2 * pair + half]
                outs.append(acc[:A_HEAD_DIM] / acc[A_HEAD_DIM:A_HEAD_DIM + 1])
            o_ref[0, rows, pair * LANES:(pair + 1) * LANES] = jnp.concatenate(outs, axis=0).T.astype(o_ref.dtype)


def _mix_kernel(h_ref, attn_ref, gmix_ref, wuv_ref, lng_ref, lnb_ref, ws_ref, bs_ref,
                wba_ref, wbg_ref, wgate_ref, bgate_ref, wout_ref, o_ref):
    rows = h_ref.shape[0] // MIX_ROW_GROUPS
    n_chunks = rows // G_CHUNK
    row = lax.broadcasted_iota(jnp.int32, (G_CHUNK, G_CHUNK), 0)
    col = lax.broadcasted_iota(jnp.int32, (G_CHUNK, G_CHUNK), 1)
    lane = lax.broadcasted_iota(jnp.int32, (G_CHUNK, n_chunks * LANES), 1)
    low_half = (lane % LANES) < G_GROUP_DIM

    def input_matmuls(r):
        sl = slice(r * rows, (r + 1) * rows)
        h = h_ref[sl, :]
        n = _rmsnorm(h, gmix_ref[...]).astype(BF16)
        return sl, h, _dot(n, wuv_ref[...]), _dot(n, wgate_ref[...])

    staged = [input_matmuls(0)]
    for r in range(MIX_ROW_GROUPS):
        if r + 1 < MIX_ROW_GROUPS:
            staged.append(input_matmuls(r + 1))
        sl, h, z_uv, gate_logits = staged[r]
        zg = jax.nn.gelu(z_uv, approximate=True)
        u = zg[:, :G_WIDTH]
        v = zg[:, G_WIDTH:]
        mu = jnp.mean(v, axis=-1, keepdims=True)
        var = jnp.mean(jnp.square(v - mu), axis=-1, keepdims=True)
        v = ((v - mu) * lax.rsqrt(var + EPS) * lng_ref[...] + lnb_ref[...]).astype(BF16)
        mixed_cols = []
        for p in range(GROUP_PAIRS):
            vp = jnp.concatenate(
                [v[c * G_CHUNK:(c + 1) * G_CHUNK, p * LANES:(p + 1) * LANES] for c in range(n_chunks)], axis=1)
            zeros = jnp.zeros_like(vp)
            rhs = jnp.concatenate([jnp.where(low_half, vp, zeros), jnp.where(low_half, zeros, vp)], axis=0)
            w_lo = jnp.where(col <= row, ws_ref[2 * p], 0.0).astype(BF16)
            w_hi = jnp.where(col <= row, ws_ref[2 * p + 1], 0.0).astype(BF16)
            mp = _dot(jnp.concatenate([w_lo, w_hi], axis=1), rhs)
            mixed_cols.append(jnp.concatenate(
                [mp[:, c * LANES:(c + 1) * LANES] for c in range(n_chunks)], axis=0))
        mixed = jnp.concatenate(mixed_cols, axis=1)
        bias = jnp.concatenate([bs_ref[...]] * n_chunks, axis=0)
        gm = (u * (mixed + bias)).astype(BF16)

        y_attn = _dot(attn_ref[sl, :], wba_ref[...])
        y_gmlp = _dot(gm, wbg_ref[...])
        gates = 0.5 * jnp.tanh(0.5 * (gate_logits + bgate_ref[...])) + 0.5
        merged = gates[:, :D_MODEL] * y_attn + gates[:, D_MODEL:] * y_gmlp
        o_ref[sl, :] = h + _dot(merged.astype(BF16), wout_ref[...])


def _resident(shape):
    zeros = (0,) * len(shape)
    return pl.BlockSpec(shape, lambda *_: zeros, pipeline_mode=pl.Buffered(1))


def _row_tiles(width, rows=TOKEN_TILE):
    return pl.BlockSpec((rows, width), lambda i: (i, 0))


def _dense_params():
    return pltpu.CompilerParams(dimension_semantics=("parallel",), vmem_limit_bytes=VMEM_LIMIT_BYTES)


def kernel(x, ffn1_norm, ffn1_w_gate, ffn1_w_up, ffn1_w_down, mix_norm, w_in, gmlp_ln_g, gmlp_ln_b, gmlp_w_s, gmlp_b_s, w_branch_attn, w_branch_gmlp, w_gate, b_gate, w_out, ffn2_norm, ffn2_w_gate, ffn2_w_up, ffn2_w_down, final_norm):
    B, S, D = x.shape
    T = B * S
    n_blocks = S // MOBA_BLOCK
    assert D == D_MODEL and S % MOBA_BLOCK == 0 and T % TOKEN_TILE == 0
    assert TOKEN_TILE % MOBA_BLOCK == 0 and TOKEN_TILE % G_CHUNK == 0
    assert n_blocks <= LANES, "one one-hot lane per key block"
    assert ffn1_norm.shape[0] == 1, "single-layer block"
    vec = lambda w: w.reshape(1, -1).astype(F32)
    n_tiles = T // TOKEN_TILE
    x2 = x.reshape(T, D)

    first = [(ffn1_w_gate, D_FF), (ffn1_w_up, D_FF), (ffn1_w_down, D), (w_in, 3 * A_WIDTH)]
    assert all(w.shape[1] % (FIRST_CAST_BLOCKS * BF16_ROWS) == 0 for w, _ in first)
    ffn1_wg, ffn1_wu, ffn1_wd, w_qkv = pl.pallas_call(
        _cast_kernel,
        grid=(FIRST_CAST_BLOCKS,),
        in_specs=[pl.BlockSpec((None, w.shape[1] // FIRST_CAST_BLOCKS, cols), lambda i: (0, i, 0))
                  for w, cols in first],
        out_specs=[pl.BlockSpec((w.shape[1] // FIRST_CAST_BLOCKS, cols), lambda i: (i, 0)) for w, cols in first],
        out_shape=[jax.ShapeDtypeStruct((w.shape[1], cols), BF16) for w, cols in first],
        compiler_params=_dense_params(),
        name="cast_first",
    )(*[w for w, _ in first])

    later = [ffn2_w_gate, ffn2_w_up, ffn2_w_down, w_gate, w_out, w_branch_attn, w_branch_gmlp]
    steps_per_block, rest = divmod(n_tiles, SIDE_CAST_BLOCKS)
    assert rest == 0 and all(w.shape[1] % (SIDE_CAST_BLOCKS * BF16_ROWS) == 0 for w in later + [w_in])
    row_block = lambda i: i // steps_per_block
    side_in_specs = [pl.BlockSpec((None, w.shape[1] // SIDE_CAST_BLOCKS, w.shape[2]), lambda i: (0, row_block(i), 0))
                     for w in later]
    side_out_specs = [pl.BlockSpec((w.shape[1] // SIDE_CAST_BLOCKS, w.shape[2]), lambda i: (row_block(i), 0))
                      for w in later]
    side_out_shapes = [jax.ShapeDtypeStruct(w.shape[1:], BF16) for w in later]
    assert (3 * A_WIDTH) % G_WIDTH == 0
    first_g = 3 * A_WIDTH // G_WIDTH
    side_in_specs += [pl.BlockSpec((None, D // SIDE_CAST_BLOCKS, G_WIDTH), lambda i, c=c: (0, row_block(i), first_g + c))
                      for c in range(2)]
    side_out_specs.append(pl.BlockSpec((D // SIDE_CAST_BLOCKS, 2 * G_WIDTH), lambda i: (row_block(i), 0)))
    side_out_shapes.append(jax.ShapeDtypeStruct((D, 2 * G_WIDTH), BF16))
    h1, qkv, kmean, *later_bf16 = pl.pallas_call(
        _ffn1_qkv_kernel,
        grid=(n_tiles,),
        in_specs=[_row_tiles(D), _resident((1, D)), _resident((D, D_FF)), _resident((D, D_FF)),
                  _resident((D_FF, D)), _resident((1, D)), _resident((D, 3 * A_WIDTH))] + side_in_specs,
        out_specs=[_row_tiles(D), _row_tiles(3 * A_WIDTH),
                   pl.BlockSpec((TOKEN_TILE // MOBA_BLOCK, 1, A_WIDTH), lambda i: (i, 0, 0))] + side_out_specs,
        out_shape=[jax.ShapeDtypeStruct((T, D), F32), jax.ShapeDtypeStruct((T, 3 * A_WIDTH), BF16),
                   jax.ShapeDtypeStruct((T // MOBA_BLOCK, 1, A_WIDTH), F32)] + side_out_shapes,
        compiler_params=pltpu.CompilerParams(dimension_semantics=("arbitrary",),
                                             vmem_limit_bytes=VMEM_LIMIT_BYTES),
        name="ffn1_qkv",
    )(x2, vec(ffn1_norm), ffn1_wg, ffn1_wu, ffn1_wd, vec(mix_norm), w_qkv, *later, w_in, w_in)
    ffn2_wg, ffn2_wu, ffn2_wd, w_gate_bf, w_out_bf, w_ba_bf, w_bg_bf, w_uv_bf = later_bf16

    qkv3 = qkv.reshape(B, S, 3 * A_WIDTH)
    q_rows = TILE_PAIR * MOBA_BLOCK
    n_chains = TILE_PAIR * A_HEADS
    assert TILE_PAIR == 2 and n_blocks % TILE_PAIR == 0
    attn = pl.pallas_call(
        _moba_kernel,
        grid=(B, n_blocks // TILE_PAIR),
        in_specs=[pl.BlockSpec((1, q_rows, A_WIDTH), lambda b, i: (b, i, 0)),
                  pl.BlockSpec((1, S, A_WIDTH), lambda b, i: (b, 0, 1)),
                  pl.BlockSpec((1, S, A_WIDTH), lambda b, i: (b, 0, 2)),
                  pl.BlockSpec((1, n_blocks, A_WIDTH), lambda b, i: (b, 0, 0))],
        out_specs=pl.BlockSpec((1, q_rows, A_WIDTH), lambda b, i: (b, i, 0)),
        out_shape=jax.ShapeDtypeStruct((B, S, A_WIDTH), BF16),
        scratch_shapes=[pltpu.VMEM((A_HEADS * V_ROWS, S), BF16),
                        pltpu.VMEM((n_chains, 2 * LANES, MOBA_BLOCK), BF16),
                        pltpu.VMEM((2, n_chains, MOBA_BLOCK, MOBA_BLOCK), F32),
                        pltpu.VMEM((2, n_chains, 1, MOBA_BLOCK), F32),
                        pltpu.VMEM((n_chains, V_ROWS, MOBA_BLOCK), F32)],
        compiler_params=pltpu.CompilerParams(dimension_semantics=("parallel", "arbitrary"),
                                             vmem_limit_bytes=VMEM_LIMIT_BYTES),
        name="moba_attn",
    )(qkv3, qkv3, qkv3, kmean.reshape(B, n_blocks, A_WIDTH))

    bs_lanes = jnp.repeat(gmlp_b_s[0].T, G_GROUP_DIM, axis=1).astype(F32)
    h2 = pl.pallas_call(
        _mix_kernel,
        grid=(T // MIX_TILE,),
        in_specs=[_row_tiles(D, MIX_TILE), _row_tiles(A_WIDTH, MIX_TILE), _resident((1, D)), _resident((D, 2 * G_WIDTH)),
                  _resident((1, G_WIDTH)), _resident((1, G_WIDTH)),
                  _resident((G_GROUPS, G_CHUNK, G_CHUNK)), _resident((G_CHUNK, G_WIDTH)),
                  _resident((A_WIDTH, D)), _resident((G_WIDTH, D)), _resident((D, 2 * D)),
                  _resident((1, 2 * D)), _resident((D, D))],
        out_specs=_row_tiles(D, MIX_TILE),
        out_shape=jax.ShapeDtypeStruct((T, D), F32),
        compiler_params=_dense_params(),
        name="mix",
    )(h1, attn.reshape(T, A_WIDTH), vec(mix_norm), w_uv_bf, vec(gmlp_ln_g),
      vec(gmlp_ln_b), gmlp_w_s[0].astype(F32), bs_lanes, w_ba_bf, w_bg_bf,
      w_gate_bf, vec(b_gate), w_out_bf)

    out = pl.pallas_call(
        _ffn2_out_kernel,
        grid=(T // FFN2_TILE,),
        in_specs=[_row_tiles(D, FFN2_TILE), _resident((1, D)), _resident((D, D_FF)), _resident((D, D_FF)),
                  _resident((D_FF, D)), _resident((1, D))],
        out_specs=_row_tiles(D, FFN2_TILE),
        out_shape=jax.ShapeDtypeStruct((T, D), F32),
        compiler_params=_dense_params(),
        name="ffn2_out",
    )(h2, vec(ffn2_norm), ffn2_wg, ffn2_wu, ffn2_wd, vec(final_norm))
    return out.reshape(B, S, D)
```
